```python
import math
import jax
import jax.numpy as jnp
from jax import lax
import numpy as np

D_MODEL = 1024
BATCH = 16
SEQ = 4096
DEPTH = 2

CTX_LEN = 256
GRID_W = 64

SWA_HEADS = 4
SWA_KV_HEADS = 2
SWA_HEAD_DIM = 64
SWA_WINDOW = 128
SWA_BLOCK = 128
DN_HEADS = 4
DN_HEAD_DIM = 64
DN_CONV = 5
DN_CHUNK = 64
RET_HEADS = 4
RET_QK_DIM = 32
RET_V_DIM = 64
RET_CHUNK = 64
MLA_HEADS = 4
MLA_Q_RANK = 256
MLA_KV_RANK = 128
MLA_NOPE_DIM = 64
MLA_ROPE_DIM = 32
MLA_V_DIM = 64
MLA_BLOCK = 128

D_FF = 4 * D_MODEL
ROPE_BASE = 10000.0
NORM_EPS = 1e-6
LN_EPS = 1e-5
DEEPNORM_ALPHA = (2 * DEPTH) ** 0.25
DEEPNORM_BETA = (8 * DEPTH) ** -0.25

SWA_Q = SWA_HEADS * SWA_HEAD_DIM
SWA_KV = SWA_KV_HEADS * SWA_HEAD_DIM
DN_W = DN_HEADS * DN_HEAD_DIM
RET_QK = RET_HEADS * RET_QK_DIM
RET_V = RET_HEADS * RET_V_DIM
MLA_OUT = MLA_HEADS * MLA_V_DIM
MIX_WIDTH = SWA_Q + DN_W + RET_V + MLA_OUT
IN_SPLITS = (SWA_Q, SWA_KV, SWA_KV, 3 * DN_W, DN_W, 4 * DN_HEADS, RET_QK, RET_QK, RET_V, RET_V,
             MLA_Q_RANK, MLA_KV_RANK, MLA_ROPE_DIM)
IN_WIDTH = sum(IN_SPLITS)

kernel_name = 'hybrid_parallel_group_dit_block'

F32 = jnp.float32


def layer_norm(x, g, b):
    xf = x.astype(F32)
    mu = jnp.mean(xf, axis=-1, keepdims=True)
    var = jnp.mean(jnp.square(xf - mu), axis=-1, keepdims=True)
    return ((xf - mu) * lax.rsqrt(var + LN_EPS) * g.astype(F32) + b.astype(F32)).astype(x.dtype)


def rms_norm(x, g):
    xf = x.astype(F32)
    return (xf * lax.rsqrt(jnp.mean(xf * xf, axis=-1, keepdims=True) + NORM_EPS) * g.astype(F32)).astype(x.dtype)


def head_layer_norm(o, g):
    b_, t_, h_, d_ = o.shape
    mu = jnp.mean(o, axis=-1, keepdims=True)
    var = jnp.mean(jnp.square(o - mu), axis=-1, keepdims=True)
    return ((o - mu) * lax.rsqrt(var + NORM_EPS)).reshape(b_, t_, h_ * d_) * g.astype(F32)


def l2norm(t):
    return t * lax.rsqrt(jnp.sum(t * t, axis=-1, keepdims=True) + NORM_EPS)


def rope_freqs(dim):
    return ROPE_BASE ** (-jnp.arange(0, dim, 2, dtype=F32) / dim)


def axial_rope(rows, rot_dim):
    row = jnp.broadcast_to(jnp.arange(rows, dtype=F32)[:, None], (rows, GRID_W)).reshape(-1)
    col = jnp.broadcast_to(jnp.arange(GRID_W, dtype=F32)[None, :], (rows, GRID_W)).reshape(-1)
    inv = rope_freqs(rot_dim // 2)
    ang = jnp.concatenate([row[:, None] * inv, col[:, None] * inv], axis=-1)
    return jnp.cos(ang), jnp.sin(ang)


def sequence_rope(n_tok, rot_dim):
    ang = jnp.arange(n_tok, dtype=F32)[:, None] * rope_freqs(rot_dim)
    return jnp.cos(ang), jnp.sin(ang)


def apply_rope(x, cos, sin):
    xf = x.astype(F32)
    x1, x2 = jnp.split(xf, 2, axis=-1)
    c = cos[:, None, :]
    s = sin[:, None, :]
    return jnp.concatenate([x1 * c - x2 * s, x1 * s + x2 * c], axis=-1).astype(x.dtype)


def _flip_t(t):
    return jnp.flip(t, axis=2)


def _split_columns(z):
    idx = np.cumsum(np.array(IN_SPLITS))[:-1].tolist()
    return jnp.split(z, idx, axis=-1)


def short_conv(x, w):
    k_width, ch = w.shape
    pad = k_width // 2
    return lax.conv_general_dilated(x, w[:, None, :].astype(x.dtype), window_strides=(1,),
                                    padding=[(pad, pad)], dimension_numbers=('NWC', 'WIO', 'NWC'),
                                    feature_group_count=ch)


def swa_group(q, k, v, qc, kc, vc, sink, cos, sin, with_ctx_out):
    b_, s_, _ = q.shape
    l_ = kc.shape[1]
    grp = SWA_HEADS // SWA_KV_HEADS
    w_ = SWA_BLOCK
    nb = s_ // w_
    d = SWA_HEAD_DIM
    scale = d ** -0.5
    q = apply_rope(q.reshape(b_, s_, SWA_HEADS, d), cos, sin).reshape(b_, nb, w_, SWA_KV_HEADS, grp, d)
    k = apply_rope(k.reshape(b_, s_, SWA_KV_HEADS, d), cos, sin)
    v = v.reshape(b_, s_, SWA_KV_HEADS, d)
    kc = kc.reshape(b_, l_, SWA_KV_HEADS, d)
    vc = vc.reshape(b_, l_, SWA_KV_HEADS, d)

    def band(t):
        tp = jnp.pad(t, ((0, 0), (w_, w_), (0, 0), (0, 0))).reshape(b_, nb + 2, w_, SWA_KV_HEADS, d)
        return jnp.concatenate([tp[:, :-2], tp[:, 1:-1], tp[:, 2:]], axis=2)

    kb, vb = band(k), band(v)
    qpos = jnp.arange(s_).reshape(nb, w_)
    kpos = (jnp.arange(nb) * w_ - w_)[:, None] + jnp.arange(3 * w_)[None, :]
    rel = kpos[:, None, :] - qpos[:, :, None]
    valid = (jnp.abs(rel) <= SWA_WINDOW) & (kpos >= 0)[:, None, :] & (kpos < s_)[:, None, :]
    sink_hg = sink.astype(F32).reshape(SWA_KV_HEADS, grp)
    s_loc = jnp.where(valid, jnp.einsum('bnqhgd,bnkhd->bhgnqk', q, kb).astype(F32) * scale, -jnp.inf)
    s_ctx = jnp.einsum('bnqhgd,bkhd->bhgnqk', q, kc).astype(F32) * scale
    s_sink = jnp.broadcast_to(sink_hg[None, :, :, None, None, None], s_ctx.shape[:-1] + (1,))
    p = jax.nn.softmax(jnp.concatenate([s_loc, s_ctx, s_sink], axis=-1), axis=-1).astype(v.dtype)
    y = (jnp.einsum('bhgnqk,bnkhd->bnqhgd', p[..., :3 * w_], vb)
         + jnp.einsum('bhgnqk,bkhd->bnqhgd', p[..., 3 * w_:3 * w_ + l_], vc)).reshape(b_, s_, SWA_Q)
    yc = None
    if with_ctx_out:
        qcg = qc.reshape(b_, l_, SWA_KV_HEADS, grp, d)
        sc = jnp.einsum('bqhgd,bkhd->bhgqk', qcg, kc).astype(F32) * scale
        ss = jnp.broadcast_to(sink_hg[None, :, :, None, None], sc.shape[:-1] + (1,))
        pc = jax.nn.softmax(jnp.concatenate([sc, ss], axis=-1), axis=-1).astype(vc.dtype)
        yc = jnp.einsum('bhgqk,bkhd->bqhgd', pc[..., :l_], vc).reshape(b_, l_, SWA_Q)
    return y, yc


def _delta_update(s, w_i, u_i, kt_i, gl_i):
    v_new = u_i - jnp.einsum('bhcd,bhde->bhce', w_i, s)
    s_new = s * jnp.exp(gl_i)[..., None, None] + jnp.einsum('bhcd,bhce->bhde', kt_i, v_new)
    return s_new, v_new


def gated_delta_chunked(q, k, v, log_g, beta, state0):
    b_, h_, t_, dk = k.shape
    dv = v.shape[-1]
    c_ = DN_CHUNK
    n = t_ // c_
    k = k.reshape(b_, h_, n, c_, dk)
    v = v.reshape(b_, h_, n, c_, dv)
    g_cum = jnp.cumsum(log_g.reshape(b_, h_, n, c_), axis=-1)
    beta = beta.reshape(b_, h_, n, c_, 1)
    incl = jnp.tril(jnp.ones((c_, c_), bool))
    strict = jnp.tril(jnp.ones((c_, c_), bool), -1)
    decay = jnp.exp(jnp.where(incl, g_cum[..., :, None] - g_cum[..., None, :], -jnp.inf))
    kb = k * beta
    a_mat = jnp.where(strict, jnp.einsum('bhncd,bhnsd->bhncs', kb, k) * decay, 0.0)
    lhs = a_mat + jnp.eye(c_, dtype=a_mat.dtype)
    rhs = jnp.concatenate([kb * jnp.exp(g_cum)[..., None], v * beta], axis=-1)
    wu = lax.linalg.triangular_solve(lhs, rhs, left_side=True, lower=True)
    w, u = wu[..., :dk], wu[..., dk:]
    g_last = g_cum[..., -1]
    k_tail = k * jnp.exp(g_last[..., None] - g_cum)[..., None]
    chunks = lambda t: jnp.moveaxis(t, 2, 0)
    if q is None:
        def step_state(s, inp):
            s_new, _ = _delta_update(s, *inp)
            return s_new, None
        s_last, _ = lax.scan(step_state, state0, (chunks(w), chunks(u), chunks(k_tail), chunks(g_last)))
        return None, s_last
    q = q.reshape(b_, h_, n, c_, dk)
    qk = jnp.einsum('bhncd,bhnsd->bhncs', q, k) * decay
    q_dec = q * jnp.exp(g_cum)[..., None]

    def step(s, inp):
        w_i, u_i, kt_i, gl_i, qd_i, qk_i = inp
        s_new, v_new = _delta_update(s, w_i, u_i, kt_i, gl_i)
        o_i = jnp.einsum('bhcd,bhde->bhce', qd_i, s) + jnp.einsum('bhcs,bhse->bhce', qk_i, v_new)
        return s_new, o_i

    s_last, o = lax.scan(step, state0, (chunks(w), chunks(u), chunks(k_tail), chunks(g_last),
                                        chunks(q_dec), chunks(qk)))
    return jnp.moveaxis(o, 0, 2).reshape(b_, h_, t_, dv), s_last


def deltanet_group(qkv, z, ab, qkv_c, z_c, ab_c, conv_w, a_log, dt_bias, norm_g, with_ctx_out):
    def prep(qkv_, ab_):
        b_, t_, _ = qkv_.shape
        y = jax.nn.silu(short_conv(qkv_, conv_w)).astype(F32)
        q, k, v = [t.reshape(b_, t_, DN_HEADS, DN_HEAD_DIM).transpose(0, 2, 1, 3) for t in jnp.split(y, 3, axis=-1)]
        q = l2norm(q) * DN_HEAD_DIM ** -0.5
        k = l2norm(k)
        ab_ = ab_.astype(F32).reshape(b_, t_, 2, 2, DN_HEADS)
        log_g = -jnp.exp(a_log.astype(F32)) * jax.nn.softplus(ab_[:, :, :, 0] + dt_bias.astype(F32))
        beta = jax.nn.sigmoid(ab_[:, :, :, 1])
        return q, k, v, log_g.transpose(2, 0, 3, 1), beta.transpose(2, 0, 3, 1)

    def out(o, z_):
        b_, t_, _ = z_.shape
        o = rms_norm(o.transpose(0, 2, 1, 3), norm_g) * jax.nn.silu(z_.astype(F32)).reshape(b_, t_, DN_HEADS, DN_HEAD_DIM)
        return o.reshape(b_, t_, DN_W).astype(z_.dtype)

    qc, kc, vc, lgc, bc = prep(qkv_c, ab_c)
    q, k, v, lg, bt = prep(qkv, ab)
    zero = jnp.zeros((qkv.shape[0], DN_HEADS, DN_HEAD_DIM, DN_HEAD_DIM), F32)
    oc_f, s_f = gated_delta_chunked(qc if with_ctx_out else None, kc, vc, lgc[0], bc[0], zero)
    oc_b, s_b = gated_delta_chunked(_flip_t(qc) if with_ctx_out else None, _flip_t(kc), _flip_t(vc),
                                    _flip_t(lgc[1]), _flip_t(bc[1]), zero)
    o_f, _ = gated_delta_chunked(q, k, v, lg[0], bt[0], s_f)
    o_b, _ = gated_delta_chunked(_flip_t(q), _flip_t(k), _flip_t(v), _flip_t(lg[1]), _flip_t(bt[1]), s_b)
    y = out(o_f + _flip_t(o_b), z)
    yc = out(oc_f + _flip_t(oc_b), z_c) if with_ctx_out else None
    return y, yc


def retention_scan(k, v, log_gamma, state0, emit_starts):
    b_, h_, t_, dk = k.shape
    c_ = RET_CHUNK
    n = t_ // c_
    pos = jnp.arange(c_, dtype=F32)
    zeta = jnp.exp((c_ - 1 - pos)[None, :] * log_gamma[:, None])
    kv = jnp.einsum('bhncd,bhnce->bhnde', k.reshape(b_, h_, n, c_, dk) * zeta[None, :, None, :, None],
                    v.reshape(b_, h_, n, c_, v.shape[-1]))
    g_chunk = jnp.exp(c_ * log_gamma)[None, :, None, None]

    def step(r, kv_i):
        return r * g_chunk + kv_i, (r if emit_starts else None)

    r_last, starts = lax.scan(step, state0, jnp.moveaxis(kv, 2, 0))
    return (jnp.moveaxis(starts, 0, 2) if emit_starts else None), r_last


def retention_readout(q, k, v, log_gamma, starts):
    b_, h_, t_, dk = q.shape
    dv = v.shape[-1]
    c_ = RET_CHUNK
    n = t_ // c_
    pos = jnp.arange(c_, dtype=F32)
    rel = pos[:, None] - pos[None, :]
    dmat = jnp.where(rel >= 0, jnp.exp(jnp.maximum(rel, 0.0)[None] * log_gamma[:, None, None]), 0.0)
    qr = q.reshape(b_, h_, n, c_, dk)
    kr = k.reshape(b_, h_, n, c_, dk)
    vr = v.reshape(b_, h_, n, c_, dv)
    inner = jnp.einsum('bhncs,bhnse->bhnce', jnp.einsum('bhncd,bhnsd->bhncs', qr, kr) * dmat[None, :, None], vr)
    xi = jnp.exp((pos + 1.0)[None, :] * log_gamma[:, None])
    cross = jnp.einsum('bhncd,bhnde->bhnce', qr * xi[None, :, None, :, None], starts)
    return (inner + cross).reshape(b_, h_, t_, dv)


def retention_group(q, k, v, g, qc, kc, vc, gc, log1m_gamma, norm_g, cos, sin, with_ctx_out):
    log_gamma = jnp.log1p(-jnp.exp(log1m_gamma.astype(F32)))
    heads = lambda t, dh: t.reshape(t.shape[0], t.shape[1], RET_HEADS, dh)
    bhtd = lambda t: t.astype(F32).transpose(0, 2, 1, 3)
    sc = RET_QK_DIM ** -0.5
    q = bhtd(apply_rope(heads(q, RET_QK_DIM), cos, sin)) * sc
    k = bhtd(apply_rope(heads(k, RET_QK_DIM), cos, sin))
    v = bhtd(heads(v, RET_V_DIM))
    kc = bhtd(heads(kc, RET_QK_DIM))
    vc = bhtd(heads(vc, RET_V_DIM))
    zero = jnp.zeros((q.shape[0], RET_HEADS, RET_QK_DIM, RET_V_DIM), F32)

    def out(o, g_):
        y = head_layer_norm(o.transpose(0, 2, 1, 3), norm_g) * jax.nn.silu(g_.astype(F32))
        return y.astype(g_.dtype)

    st_cf, r_f = retention_scan(kc, vc, log_gamma[0], zero, with_ctx_out)
    st_cb, r_b = retention_scan(_flip_t(kc), _flip_t(vc), log_gamma[1], zero, with_ctx_out)
    st_f, _ = retention_scan(k, v, log_gamma[0], r_f, True)
    st_b, _ = retention_scan(_flip_t(k), _flip_t(v), log_gamma[1], r_b, True)
    o = (retention_readout(q, k, v, log_gamma[0], st_f)
         + _flip_t(retention_readout(_flip_t(q), _flip_t(k), _flip_t(v), log_gamma[1], st_b)))
    y = out(o, g)
    yc = None
    if with_ctx_out:
        qcs = bhtd(heads(qc, RET_QK_DIM)) * sc
        oc = (retention_readout(qcs, kc, vc, log_gamma[0], st_cf)
              + _flip_t(retention_readout(_flip_t(qcs), _flip_t(kc), _flip_t(vc), log_gamma[1], st_cb)))
        yc = out(oc, gc)
    return y, yc


def mla_queries(cq, q_norm, w_uq, cos, sin):
    b_, t_, _ = cq.shape
    q = (rms_norm(cq, q_norm) @ w_uq).reshape(b_, t_, MLA_HEADS, MLA_NOPE_DIM + MLA_ROPE_DIM)
    if cos is None:
        return q
    return jnp.concatenate([q[..., :MLA_NOPE_DIM], apply_rope(q[..., MLA_NOPE_DIM:], cos, sin)], axis=-1)


def mla_keys_values(ckv, kr, kv_norm, w_ukv, cos, sin):
    b_, t_, _ = ckv.shape
    kv = (rms_norm(ckv, kv_norm) @ w_ukv).reshape(b_, t_, MLA_HEADS, MLA_NOPE_DIM + MLA_V_DIM)
    kr = kr[:, :, None, :]
    if cos is not None:
        kr = apply_rope(kr, cos, sin)
    k = jnp.concatenate([kv[..., :MLA_NOPE_DIM], jnp.broadcast_to(kr, (b_, t_, MLA_HEADS, MLA_ROPE_DIM))], axis=-1)
    return k, kv[..., MLA_NOPE_DIM:]


def mla_attend(q, k_all, v_all):
    b_, t_, h_, dq = q.shape
    nb = t_ // MLA_BLOCK
    scale = dq ** -0.5
    qb = q.reshape(b_, nb, MLA_BLOCK, h_, dq).transpose(1, 0, 2, 3, 4)

    def one_block(qi):
        s = jnp.einsum('bqhd,bkhd->bhqk', qi, k_all).astype(F32) * scale
        p = jax.nn.softmax(s, axis=-1).astype(v_all.dtype)
        return jnp.einsum('bhqk,bkhd->bqhd', p, v_all)

    o = lax.map(one_block, qb)
    return o.transpose(1, 0, 2, 3, 4).reshape(b_, t_, h_ * v_all.shape[-1])


def mla_group(cq, ckv, kr, cq_c, ckv_c, kr_c, q_norm, w_uq, kv_norm, w_ukv, cos, sin, with_ctx_out):
    q = mla_queries(cq, q_norm, w_uq, cos, sin)
    k, v = mla_keys_values(ckv, kr, kv_norm, w_ukv, cos, sin)
    kc, vc = mla_keys_values(ckv_c, kr_c, kv_norm, w_ukv, None, None)
    y = mla_attend(q, jnp.concatenate([k, kc], axis=1), jnp.concatenate([v, vc], axis=1))
    yc = mla_attend(mla_queries(cq_c, q_norm, w_uq, None, None), kc, vc) if with_ctx_out else None
    return y, yc


def token_mixers(h, hc, w_in, swa_sink, dn_conv_w, dn_a_log, dn_dt_bias, dn_norm_g, ret_log1m_gamma,
                 ret_norm_g, mla_q_norm, mla_w_uq, mla_kv_norm, mla_w_ukv, rope, with_ctx_out):
    (a_q, a_k, a_v, b_qkv, b_z, b_ab, c_q, c_k, c_v, c_g, d_cq, d_ckv, d_kr) = _split_columns(h @ w_in)
    (a_qc, a_kc, a_vc, b_qkvc, b_zc, b_abc, c_qc, c_kc, c_vc, c_gc, d_cqc, d_ckvc, d_krc) = _split_columns(hc @ w_in)
    swa_cos, swa_sin, ret_cos, ret_sin, mla_cos, mla_sin = rope
    ya, yac = swa_group(a_q, a_k, a_v, a_qc, a_kc, a_vc, swa_sink, swa_cos, swa_sin, with_ctx_out)
    yb, ybc = deltanet_group(b_qkv, b_z, b_ab, b_qkvc, b_zc, b_abc, dn_conv_w, dn_a_log, dn_dt_bias,
                             dn_norm_g, with_ctx_out)
    yr, yrc = retention_group(c_q, c_k, c_v, c_g, c_qc, c_kc, c_vc, c_gc, ret_log1m_gamma, ret_norm_g,
                              ret_cos, ret_sin, with_ctx_out)
    yd, ydc = mla_group(d_cq, d_ckv, d_kr, d_cqc, d_ckvc, d_krc, mla_q_norm, mla_w_uq, mla_kv_norm,
                        mla_w_ukv, mla_cos, mla_sin, with_ctx_out)
    y = jnp.concatenate([ya, yb, yr, yd], axis=-1)
    yc = jnp.concatenate([yac, ybc, yrc, ydc], axis=-1) if with_ctx_out else None
    return y, yc


def squared_relu_mlp(h, w1, w2):
    return jnp.square(jax.nn.relu(h @ w1)) @ w2


def setup_inputs(seed: int = 0) -> dict:
    key = jax.random.key(seed)
    ks = jax.random.split(key, 32)
    nrm = lambda k, shape, scale: jax.random.normal(k, shape, F32) * scale
    dt = jnp.exp(jax.random.uniform(ks[10], (DEPTH, 2, DN_HEADS), F32, math.log(1e-3), math.log(1e-1)))
    return {
        'x': nrm(ks[0], (BATCH, SEQ, D_MODEL), 1.0),
        'c': nrm(ks[1], (BATCH, D_MODEL), 1.0),
        'ctx': nrm(ks[2], (BATCH, CTX_LEN, D_MODEL), 1.0),
        'c_ctx': nrm(ks[3], (D_MODEL,), 1.0),
        'ada_w': nrm(ks[4], (DEPTH, D_MODEL, 6 * D_MODEL), D_MODEL ** -0.5),
        'ada_b': nrm(ks[5], (DEPTH, 6 * D_MODEL), 0.02),
        'w_in': nrm(ks[6], (DEPTH, D_MODEL, IN_WIDTH), D_MODEL ** -0.5),
        'swa_sink': nrm(ks[7], (DEPTH, SWA_HEADS), 0.5),
        'dn_conv_w': nrm(ks[8], (DEPTH, DN_CONV, 3 * DN_W), DN_CONV ** -0.5),
        'dn_a_log': jnp.log(jax.random.uniform(ks[9], (DEPTH, 2, DN_HEADS), F32, 1.0, 16.0)),
        'dn_dt_bias': dt + jnp.log(-jnp.expm1(-dt)),
        'dn_norm_g': 1.0 + nrm(ks[11], (DEPTH, DN_HEAD_DIM), 0.02),
        'ret_log1m_gamma': (-(5.0 + jnp.arange(RET_HEADS, dtype=F32)) * math.log(2.0)
                            + nrm(ks[12], (DEPTH, 2, RET_HEADS), 0.05)),
        'ret_norm_g': 1.0 + nrm(ks[13], (DEPTH, RET_V), 0.02),
        'mla_q_norm': 1.0 + nrm(ks[14], (DEPTH, MLA_Q_RANK), 0.02),
        'mla_w_uq': nrm(ks[15], (DEPTH, MLA_Q_RANK, MLA_HEADS * (MLA_NOPE_DIM + MLA_ROPE_DIM)), MLA_Q_RANK ** -0.5),
        'mla_kv_norm': 1.0 + nrm(ks[16], (DEPTH, MLA_KV_RANK), 0.02),
        'mla_w_ukv': nrm(ks[17], (DEPTH, MLA_KV_RANK, MLA_HEADS * (MLA_NOPE_DIM + MLA_V_DIM)), MLA_KV_RANK ** -0.5),
        'w_out': nrm(ks[18], (DEPTH, MIX_WIDTH, D_MODEL), MIX_WIDTH ** -0.5 * DEEPNORM_BETA),
        'ln1_g': 1.0 + nrm(ks[19], (DEPTH, D_MODEL), 0.02),
        'ln1_b': nrm(ks[20], (DEPTH, D_MODEL), 0.02),
        'w_ff1': nrm(ks[21], (DEPTH, D_MODEL, D_FF), D_MODEL ** -0.5),
        'w_ff2': nrm(ks[22], (DEPTH, D_FF, D_MODEL), D_FF ** -0.5 * DEEPNORM_BETA),
        'ln2_g': 1.0 + nrm(ks[23], (DEPTH, D_MODEL), 0.02),
        'ln2_b': nrm(ks[24], (DEPTH, D_MODEL), 0.02),
    }


def reference(x, c, ctx, c_ctx, ada_w, ada_b, w_in, swa_sink, dn_conv_w, dn_a_log, dn_dt_bias, dn_norm_g,
              ret_log1m_gamma, ret_norm_g, mla_q_norm, mla_w_uq, mla_kv_norm, mla_w_ukv, w_out, ln1_g, ln1_b,
              w_ff1, w_ff2, ln2_g, ln2_b):
    n_tok = x.shape[1]
    rows = n_tok // GRID_W
    rope = (*axial_rope(rows, SWA_HEAD_DIM), *sequence_rope(n_tok, RET_QK_DIM), *axial_rope(rows, MLA_ROPE_DIM))
    silu_c = jax.nn.silu(c)
    silu_cc = jax.nn.silu(c_ctx)
    xc = ctx
    for layer in range(DEPTH):
        with_ctx_out = layer < DEPTH - 1
        sh1, sc1, g1, sh2, sc2, g2 = jnp.split((silu_c @ ada_w[layer] + ada_b[layer])[:, None, :], 6, axis=-1)
        csh1, csc1, cg1, csh2, csc2, cg2 = jnp.split((silu_cc @ ada_w[layer] + ada_b[layer])[None, None, :], 6, axis=-1)
        y, yc = token_mixers(x * (1 + sc1) + sh1, xc * (1 + csc1) + csh1, w_in[layer], swa_sink[layer],
                             dn_conv_w[layer], dn_a_log[layer], dn_dt_bias[layer], dn_norm_g[layer],
                             ret_log1m_gamma[layer], ret_norm_g[layer], mla_q_norm[layer], mla_w_uq[layer],
                             mla_kv_norm[layer], mla_w_ukv[layer], rope, with_ctx_out)
        x = layer_norm(DEEPNORM_ALPHA * x + g1 * (y @ w_out[layer]), ln1_g[layer], ln1_b[layer])
        x = layer_norm(DEEPNORM_ALPHA * x + g2 * squared_relu_mlp(x * (1 + sc2) + sh2, w_ff1[layer], w_ff2[layer]),
                       ln2_g[layer], ln2_b[layer])
        if with_ctx_out:
            xc = layer_norm(DEEPNORM_ALPHA * xc + cg1 * (yc @ w_out[layer]), ln1_g[layer], ln1_b[layer])
            xc = layer_norm(DEEPNORM_ALPHA * xc + cg2 * squared_relu_mlp(xc * (1 + csc2) + csh2, w_ff1[layer], w_ff2[layer]),
                            ln2_g[layer], ln2_b[layer])
    return x
```

```python
import functools
import math

import numpy as np
import jax
import jax.numpy as jnp
from jax import lax
from jax.experimental import pallas as pl
from jax.experimental.pallas import tpu as pltpu

F32 = jnp.float32
BF16 = jnp.bfloat16

D_MODEL = 1024
DEPTH = 2
GRID_W = 64
SWA_HEADS, SWA_KV_HEADS, SWA_HEAD_DIM, SWA_WINDOW = 4, 2, 64, 128
DN_HEADS, DN_HEAD_DIM, DN_CONV, DN_CHUNK = 4, 64, 5, 64
RET_HEADS, RET_QK_DIM, RET_V_DIM = 4, 32, 64
MLA_HEADS, MLA_Q_RANK, MLA_KV_RANK, MLA_NOPE_DIM, MLA_ROPE_DIM, MLA_V_DIM = 4, 256, 128, 64, 32, 64
D_FF = 4 * D_MODEL
ROPE_BASE = 10000.0
NORM_EPS = 1e-6
LN_EPS = 1e-5
DEEPNORM_ALPHA = (2 * DEPTH) ** 0.25
IN_SPLITS = (256, 128, 128, 768, 256, 16, 128, 128, 256, 256, 256, 128, 32)

ZA_W, ZB_W, ZC_W, ZD_W, KRS_W = 512, 1152, 768, 512, 128
Z_OFF = np.cumsum((0, ZA_W, ZB_W, ZC_W, ZD_W, KRS_W))
Z_W = int(Z_OFF[-1])

VMEM_LIMIT = 48 * 1024 * 1024
NEG_BIG = -1e30

DN_TILE = 256
RET_TILE = 256
MLA_TQ = 256
TOK_TILE = 256


def _dot(a, b):
    return jnp.dot(a.astype(BF16), b.astype(BF16), preferred_element_type=F32)


def _dot_nt(a, b):
    return lax.dot_general(a.astype(BF16), b.astype(BF16), (((1,), (1,)), ((), ())), preferred_element_type=F32)


def _dot_tn(a, b):
    return lax.dot_general(a.astype(BF16), b.astype(BF16), (((0,), (0,)), ((), ())), preferred_element_type=F32)


def _split2(a):
    hi = a.astype(BF16)
    lo = (a - hi.astype(F32)).astype(BF16)
    return hi, lo


def _split3(a):
    hi = a.astype(BF16)
    r = a - hi.astype(F32)
    mid = r.astype(BF16)
    lo = (r - mid.astype(F32)).astype(BF16)
    return hi, mid, lo


def _dot_sel(a, sel):
    return sum(jnp.dot(p, sel, preferred_element_type=F32) for p in _split3(a))


def _dot_sel_nt(sel, a):
    dn = (((1,), (1,)), ((), ()))
    return sum(lax.dot_general(sel, p, dn, preferred_element_type=F32) for p in _split3(a))


def _dot_hp(a, b):
    ah, al = _split2(a)
    bh, bl = _split2(b)
    f = lambda x, y: jnp.dot(x, y, preferred_element_type=F32)
    return f(ah, bh) + (f(ah, bl) + f(al, bh))


def _group_ones(width, shift):
    r = lax.broadcasted_iota(jnp.int32, (width, width), 0) >> shift
    c = lax.broadcasted_iota(jnp.int32, (width, width), 1) >> shift
    return jnp.where(r == c, 1.0, 0.0).astype(BF16)


def _group_sum(x, ones_bd):
    hi, lo = _split2(x)
    return jnp.dot(hi, ones_bd, preferred_element_type=F32) + jnp.dot(lo, ones_bd, preferred_element_type=F32)


def _silu(x):
    return x * jax.nn.sigmoid(x)


def _layer_norm(v, g, b):
    mu = jnp.mean(v, axis=-1, keepdims=True)
    d = v - mu
    var = jnp.mean(d * d, axis=-1, keepdims=True)
    return d * lax.rsqrt(var + LN_EPS) * g + b


def _lane_iota(width):
    return lax.broadcasted_iota(jnp.int32, (1, width), 1)


def _params(*sem):
    return pltpu.CompilerParams(dimension_semantics=sem, vmem_limit_bytes=VMEM_LIMIT)


def _const_spec(shape):
    nd = len(shape)
    return pl.BlockSpec(shape, lambda *_: (0,) * nd)


def _ada_kernel(c_ref, w_ref, b_ref, o_ref):
    o_ref[...] = _dot(_silu(c_ref[...]), w_ref[...]) + b_ref[...]


def _ada_modulation(cc, w, b):
    rows, d = cc.shape
    n = w.shape[1]
    tn = 1536
    return pl.pallas_call(
        _ada_kernel,
        name="ada_modulation",
        grid=(n // tn,),
        in_specs=[pl.BlockSpec((rows, d), lambda j: (0, 0)), pl.BlockSpec((d, tn), lambda j: (0, j)),
                  pl.BlockSpec((1, tn), lambda j: (0, j))],
        out_specs=pl.BlockSpec((rows, tn), lambda j: (0, j)),
        out_shape=jax.ShapeDtypeStruct((rows, n), F32),
        compiler_params=_params("parallel"),
    )(cc, w, b.reshape(1, n))


def _in_proj_columns():
    off = np.cumsum((0,) + IN_SPLITS)
    aq, ak, av, bqkv, bz, bab, cq, ck, cv, cg, dcq, dckv, dkr = (int(o) for o in off[:-1])
    ar = lambda base, n: list(range(base, base + n))
    cols = []
    for g in range(2):
        for half in range(2):
            for h in range(2):
                cols += ar(aq + (2 * h + g) * 64 + half * 32, 32)
    for half in range(2):
        for h in range(2):
            cols += ar(ak + h * 64 + half * 32, 32)
    cols += ar(av, 128)
    cols += ar(bqkv, 768) + ar(bz, 256) + ar(bab, 16) + [-1] * 112
    for base in (cq, ck):
        for half in range(2):
            for h in range(4):
                cols += ar(base + h * 32 + half * 16, 16)
    cols += ar(cv, 256) + ar(cg, 256)
    cols += ar(dcq, 256) + ar(dckv, 128)
    cols += [-1] * 64 + ar(dkr, 32) + [-1] * 32
    cols += [-1] * 64 + ar(dkr + 16, 16) + ar(dkr, 16) + [-1] * 32
    cols = np.asarray(cols, np.int32)
    assert cols.shape[0] == Z_W
    return cols


_IN_COLS = _in_proj_columns()


def _permute_in_proj(w_in):
    idx = jnp.asarray(np.maximum(_IN_COLS, 0))
    keep = jnp.asarray((_IN_COLS >= 0).astype(np.float32))
    return (jnp.take(w_in, idx, axis=1) * keep[None, :]).astype(BF16)


def _in_proj_kernel(*refs, rope):
    if rope:
        x_ref, sc_ref, sh_ref, w_ref, ca_ref, sa_ref, cc_ref, sc2_ref, cd_ref, sd_ref, oa, ob, oc, od = refs
    else:
        x_ref, sc_ref, sh_ref, w_ref, oa, ob, oc, od = refs
    h = x_ref[0] * (1.0 + sc_ref[0]) + sh_ref[0]
    z = _dot(h, w_ref[...])
    za = z[:, Z_OFF[0]:Z_OFF[1]]
    zc = z[:, Z_OFF[2]:Z_OFF[3]]
    zd = z[:, Z_OFF[3]:Z_OFF[4]]
    ob[0] = z[:, Z_OFF[1]:Z_OFF[2]]
    if not rope:
        oa[0] = za
        oc[0] = zc
        od[0] = zd
        return
    krs = z[:, Z_OFF[4]:Z_OFF[5]]

    def rot(t, c, s):
        return t * c + pltpu.roll(t, 64, 1) * s

    ca, sa = ca_ref[...], sa_ref[...]
    for j in range(3):
        oa[0, :, j * 128:(j + 1) * 128] = rot(za[:, j * 128:(j + 1) * 128], ca, sa)
    oa[0, :, 384:512] = za[:, 384:512]
    cc, sc2 = cc_ref[...], sc2_ref[...]
    for j in range(2):
        oc[0, :, j * 128:(j + 1) * 128] = rot(zc[:, j * 128:(j + 1) * 128], cc, sc2)
    oc[0, :, 256:768] = zc[:, 256:768]
    od[0, :, 0:384] = zd[:, 0:384]
    od[0, :, 384:512] = zd[:, 384:512] * cd_ref[...] + krs * sd_ref[...]


def _in_proj(x, sc, sh, w, tables):
    b, t, d = x.shape
    tm = min(TOK_TILE, t)
    rope = tables is not None
    tok = lambda w_: pl.BlockSpec((1, tm, w_), lambda i, j: (i, j, 0))
    vec = pl.BlockSpec((1, 1, d), lambda i, j: (i, 0, 0))
    tab = pl.BlockSpec((tm, 128), lambda i, j: (j, 0))
    in_specs = [tok(d), vec, vec, _const_spec((d, Z_W))] + ([tab] * 6 if rope else [])
    args = [x, sc, sh, w] + (list(tables) if rope else [])
    widths = (ZA_W, ZB_W, ZC_W, ZD_W)
    return pl.pallas_call(
        functools.partial(_in_proj_kernel, rope=rope),
        name="in_proj_rope" if rope else "in_proj_ctx",
        grid=(b, t // tm),
        in_specs=in_specs,
        out_specs=[tok(w_) for w_ in widths],
        out_shape=[jax.ShapeDtypeStruct((b, t, w_), F32) for w_ in widths],
        compiler_params=_params("parallel", "parallel"),
    )(*args)


def _swa_heads(sink_ref, qg, keys, vals, mask):
    lane = _lane_iota(128)
    scale = SWA_HEAD_DIM ** -0.5
    outs = []
    for g in range(2):
        acc = None
        for h in range(2):
            qm = jnp.where(((lane >> 5) & 1) == h, qg[g], 0.0)
            s = _dot_nt(qm, keys) * scale
            if mask is not None:
                s = jnp.where(mask, s, NEG_BIG)
            snk = sink_ref[2 * h + g]
            m = jnp.maximum(jnp.max(s, axis=-1, keepdims=True), snk)
            p = jnp.exp(s - m)
            den = jnp.sum(p, axis=-1, keepdims=True) + jnp.exp(snk - m)
            o = _dot(p, vals) * jnp.where((lane >> 6) == h, 1.0 / den, 0.0)
            acc = o if acc is None else acc + o
        outs.append(acc)
    return jnp.concatenate(outs, axis=-1)


def _swa_local_kernel(sink_ref, q_ref, kp_ref, k0_ref, kn_ref, vp_ref, v0_ref, vn_ref, kc_ref, vc_ref, o_ref, *, nb):
    n = pl.program_id(1)
    w = SWA_WINDOW
    keys = jnp.concatenate([kp_ref[0], k0_ref[0], kn_ref[0], kc_ref[0]], axis=0)
    vals = jnp.concatenate([vp_ref[0], v0_ref[0], vn_ref[0], vc_ref[0]], axis=0)
    nk = keys.shape[0]
    i = lax.broadcasted_iota(jnp.int32, (w, nk), 0)
    j = lax.broadcasted_iota(jnp.int32, (w, nk), 1)
    lo = jnp.where(n > 0, 0, w)
    hi = jnp.where(n < nb - 1, 3 * w, 2 * w)
    band = (j >= i) & (j <= i + 2 * w) & (j >= lo) & (j < hi)
    mask = band | (j >= 3 * w)
    q = q_ref[0]
    o_ref[0] = _swa_heads(sink_ref, (q[:, :128], q[:, 128:]), keys, vals, mask)


def _swa_ctx_kernel(sink_ref, q_ref, kc_ref, vc_ref, o_ref):
    q = q_ref[0]
    o_ref[0] = _swa_heads(sink_ref, (q[:, :128], q[:, 128:]), kc_ref[0], vc_ref[0], None)


_SMEM_SPEC = pl.BlockSpec(memory_space=pltpu.SMEM)


def _swa_local(za, zac, sink):
    b, s, _ = za.shape
    l = zac.shape[1]
    w = SWA_WINDOW
    nb = s // w
    blk = lambda col, f: pl.BlockSpec((1, w, 128), lambda i, n: (i, f(n), col))
    prev = lambda n: jnp.maximum(n - 1, 0)
    same = lambda n: n
    nxt = lambda n: jnp.minimum(n + 1, nb - 1)
    ctx = lambda col: pl.BlockSpec((1, l, 128), lambda i, n: (i, 0, col))
    return pl.pallas_call(
        functools.partial(_swa_local_kernel, nb=nb),
        name="swa_local",
        grid=(b, nb),
        in_specs=[_SMEM_SPEC, pl.BlockSpec((1, w, 256), lambda i, n: (i, n, 0)),
                  blk(2, prev), blk(2, same), blk(2, nxt), blk(3, prev), blk(3, same), blk(3, nxt), ctx(2), ctx(3)],
        out_specs=pl.BlockSpec((1, w, 256), lambda i, n: (i, n, 0)),
        out_shape=jax.ShapeDtypeStruct((b, s, 256), F32),
        compiler_params=_params("parallel", "parallel"),
    )(sink, za, za, za, za, za, za, za, zac, zac)


def _swa_ctx(zac, sink):
    b, l, _ = zac.shape
    ctx = lambda col, w_: pl.BlockSpec((1, l, w_), lambda i: (i, 0, col))
    return pl.pallas_call(
        _swa_ctx_kernel,
        name="swa_ctx",
        grid=(b,),
        in_specs=[_SMEM_SPEC, ctx(0, 256), ctx(2, 128), ctx(3, 128)],
        out_specs=pl.BlockSpec((1, l, 256), lambda i: (i, 0, 0)),
        out_shape=jax.ShapeDtypeStruct((b, l, 256), F32),
        compiler_params=_params("parallel"),
    )(sink, zac, zac, zac)


def _dn_prepare(x, prev8, next8, cw_ref, alog, dtb, xext, q_s, k_s, v_s, lg_s, beta_s, *, direction):
    t = x.shape[0]
    xext[0:8, :] = prev8
    xext[8:8 + t, :] = x[:, 0:768]
    xext[8 + t:16 + t, :] = next8
    pad = DN_CONV // 2
    y = None
    for kk in range(DN_CONV):
        term = cw_ref[kk:kk + 1, :] * xext[pl.ds(8 - pad + kk, t), :]
        y = term if y is None else y + term
    y = _silu(y)
    ones64 = _group_ones(256, 6)
    q, k, v = y[:, 0:256], y[:, 256:512], y[:, 512:768]
    q_s[...] = q * lax.rsqrt(_group_sum(q * q, ones64) + NORM_EPS) * (DN_HEAD_DIM ** -0.5)
    k_s[...] = k * lax.rsqrt(_group_sum(k * k, ones64) + NORM_EPS)
    v_s[...] = v
    ab = x[:, 1024:1152]
    r = lax.broadcasted_iota(jnp.int32, (128, 256), 0)
    hd = lax.broadcasted_iota(jnp.int32, (128, 256), 1) >> 6
    sel_g = jnp.where(r == direction * 8 + hd, 1.0, 0.0).astype(BF16)
    sel_b = jnp.where(r == direction * 8 + 4 + hd, 1.0, 0.0).astype(BF16)
    graw = _dot_sel(ab, sel_g) + dtb
    softplus = jnp.maximum(graw, 0.0) + jnp.log1p(jnp.exp(-jnp.abs(graw)))
    lg_s[...] = -jnp.exp(alog) * softplus
    beta_s[...] = jax.nn.sigmoid(_dot_sel(ab, sel_b))


def _unit_triangular_inverse(a, ii, jj, eye):
    same = lambda shift: (ii >> shift) == (jj >> shift)
    a8 = jnp.where(same(3), a, 0.0)
    t = eye - a8
    pw = a8
    for _ in range(2):
        pw = _dot_hp(pw, pw)
        t = t + _dot_hp(t, pw)
    for shift in (4, 5, 6):
        off = jnp.where(same(shift) & jnp.logical_not(same(shift - 1)), a, 0.0)
        t = t - _dot_hp(_dot_hp(t, off), t)
    return t


def _dn_chunk(ci, q_s, k_s, v_s, lg_s, beta_s, o_s, st, *, reverse, emit):
    c = DN_CHUNK
    rows = pl.ds(pl.multiple_of(ci * c, c), c)
    k = k_s[rows, :]
    v = v_s[rows, :]
    lg = lg_s[rows, :]
    beta = beta_s[rows, :]
    ii = lax.broadcasted_iota(jnp.int32, (c, c), 0)
    jj = lax.broadcasted_iota(jnp.int32, (c, c), 1)
    incl = (jj >= ii) if reverse else (jj <= ii)
    strict = (jj > ii) if reverse else (jj < ii)
    tri = jnp.where(incl, 1.0, 0.0).astype(BF16)
    gc = sum(jnp.dot(tri, p, preferred_element_type=F32) for p in _split3(lg))
    last = 0 if reverse else c - 1
    glast = gc[last:last + 1, :]
    eg = jnp.exp(gc)
    kb = k * beta
    kbe = kb * eg
    vb = v * beta
    ktail = k * jnp.exp(glast - gc)
    lane = _lane_iota(256)
    eye = jnp.where(ii == jj, 1.0, 0.0)
    lrow = lax.broadcasted_iota(jnp.int32, (256, c), 0)
    lcol = lax.broadcasted_iota(jnp.int32, (c, 256), 1)
    gparts = _split3(gc)
    w_all = None
    u_all = None
    qk_list = []
    if emit:
        q = q_s[rows, :]
        qdec = q * eg
    for h in range(DN_HEADS):
        hm = (lane >> 6) == h
        pick_col = jnp.where(lrow == 64 * h, 1.0, 0.0).astype(BF16)
        pick_row = jnp.where(lcol == 64 * h, 1.0, 0.0).astype(BF16)
        colb = sum(jnp.dot(p, pick_col, preferred_element_type=F32) for p in gparts)
        rowb = sum(lax.dot_general(pick_row, p, (((1,), (1,)), ((), ())), preferred_element_type=F32)
                   for p in gparts)
        decay = jnp.exp(jnp.where(incl, colb - rowb, NEG_BIG))
        a = jnp.where(strict, _dot_nt(jnp.where(hm, kb, 0.0), k) * decay, 0.0)
        tinv = _unit_triangular_inverse(a, ii, jj, eye)
        w_h = jnp.where(hm, _dot_hp(tinv, kbe), 0.0)
        u_h = jnp.where(hm, _dot_hp(tinv, vb), 0.0)
        w_all = w_h if w_all is None else w_all + w_h
        u_all = u_h if u_all is None else u_all + u_h
        if emit:
            qk_list.append(_dot_nt(jnp.where(hm, q, 0.0), k) * decay)
    s_old = st[...]
    vnew = u_all - _dot(w_all, s_old)
    if emit:
        o = _dot(qdec, s_old)
        for h in range(DN_HEADS):
            o = o + jnp.where((lane >> 6) == h, _dot(qk_list[h], vnew), 0.0)
        o_s[rows, :] = o
    rr = lax.broadcasted_iota(jnp.int32, (256, 256), 0) >> 6
    cc = lax.broadcasted_iota(jnp.int32, (256, 256), 1) >> 6
    st[...] = s_old * jnp.exp(glast) + jnp.where(rr == cc, _dot_tn(ktail, vnew), 0.0)


def _dn_kernel(*refs, reverse, nt, with_ctx_out, final):
    it = iter(refs)
    xc_ref, x_ref, xp_ref, xn_ref, cw_ref, alog_ref, dtb_ref = (next(it) for _ in range(7))
    if final:
        g_ref, of_ref = next(it), next(it)
        ofc_ref = next(it) if with_ctx_out else None
    o_ref = next(it)
    oc_ref = next(it) if with_ctx_out else None
    xext, q_s, k_s, v_s, lg_s, beta_s, o_s, st = (next(it) for _ in range(8))
    s = pl.program_id(1)
    direction = 1 if reverse else 0
    nchunk = DN_TILE // DN_CHUNK
    alog, dtb = alog_ref[...], dtb_ref[...]

    def run(x, prev8, next8, emit):
        _dn_prepare(x, prev8, next8, cw_ref, alog, dtb, xext, q_s, k_s, v_s, lg_s, beta_s, direction=direction)

        def body(i, carry):
            ci = (nchunk - 1 - i) if reverse else i
            _dn_chunk(ci, q_s, k_s, v_s, lg_s, beta_s, o_s, st, reverse=reverse, emit=emit)
            return carry

        lax.fori_loop(0, nchunk, body, 0)

    def finish(o, x, other_ref, out_ref):
        if not final:
            out_ref[0] = o
            return
        o = o + other_ref[0]
        ms = _group_sum(o * o, _group_ones(256, 6)) * (1.0 / DN_HEAD_DIM)
        out_ref[0] = o * lax.rsqrt(ms + NORM_EPS) * g_ref[...] * _silu(x[:, 768:1024])

    zeros8 = jnp.zeros((8, 768), F32)

    @pl.when(s == 0)
    def _():
        st[...] = jnp.zeros_like(st)
        xc = xc_ref[0]
        run(xc, zeros8, zeros8, with_ctx_out)
        if with_ctx_out:
            finish(o_s[...], xc, ofc_ref if final else None, oc_ref)

    @pl.when(s > 0)
    def _():
        tile = (nt - s) if reverse else (s - 1)
        x = x_ref[0]
        prev8 = jnp.where(tile > 0, xp_ref[0], 0.0)
        next8 = jnp.where(tile < nt - 1, xn_ref[0], 0.0)
        run(x, prev8, next8, True)
        finish(o_s[...], x, of_ref if final else None, o_ref)


def _deltanet_direction(zb, zbc, conv_w, alog_x, dtb_x, norm_g, other, other_c, *, reverse, with_ctx_out):
    b, s, _ = zb.shape
    l = zbc.shape[1]
    tt = DN_TILE
    assert l == tt and s % tt == 0
    nt = s // tt
    final = other is not None
    r8 = tt // 8
    tile_of = (lambda st_: nt - 1 - jnp.maximum(st_ - 1, 0)) if reverse else (lambda st_: jnp.maximum(st_ - 1, 0))
    tok = lambda w_: pl.BlockSpec((1, tt, w_), lambda i, j: (i, tile_of(j), 0))
    ctx = lambda w_: pl.BlockSpec((1, l, w_), lambda i, j: (i, 0, 0))
    halo_p = pl.BlockSpec((1, 8, 768), lambda i, j: (i, jnp.maximum(tile_of(j) * r8 - 1, 0), 0))
    halo_n = pl.BlockSpec((1, 8, 768), lambda i, j: (i, jnp.minimum((tile_of(j) + 1) * r8, s // 8 - 1), 0))
    in_specs = [ctx(ZB_W), tok(ZB_W), halo_p, halo_n, _const_spec((8, 768)), _const_spec((1, 256)), _const_spec((1, 256))]
    args = [zbc, zb, zb, zb, conv_w, alog_x, dtb_x]
    if final:
        in_specs += [_const_spec((1, 256)), tok(256)]
        args += [norm_g, other]
        if with_ctx_out:
            in_specs.append(ctx(256))
            args.append(other_c)
    out_specs = [tok(256)]
    out_shape = [jax.ShapeDtypeStruct((b, s, 256), F32)]
    if with_ctx_out:
        out_specs.append(ctx(256))
        out_shape.append(jax.ShapeDtypeStruct((b, l, 256), F32))
    scratch = [pltpu.VMEM((tt + 16, 768), F32)] + [pltpu.VMEM((tt, 256), F32)] * 6 + [pltpu.VMEM((256, 256), F32)]
    outs = pl.pallas_call(
        functools.partial(_dn_kernel, reverse=reverse, nt=nt, with_ctx_out=with_ctx_out, final=final),
        name="deltanet_bwd" if reverse else "deltanet_fwd",
        grid=(b, nt + 1),
        in_specs=in_specs,
        out_specs=out_specs,
        out_shape=out_shape,
        scratch_shapes=scratch,
        compiler_params=_params("parallel", "arbitrary"),
    )(*args)
    return (outs[0], outs[1]) if with_ctx_out else (outs[0], None)


def _deltanet(zb, zbc, conv_w, a_log, dt_bias, norm_g, with_ctx_out):
    cw = jnp.pad(conv_w, ((0, 8 - DN_CONV), (0, 0)))
    expand = lambda p: jnp.repeat(p, DN_HEAD_DIM, axis=-1)
    alog_x, dtb_x = expand(a_log), expand(dt_bias)
    g = jnp.tile(norm_g, DN_HEADS).reshape(1, 256)
    o_f, oc_f = _deltanet_direction(zb, zbc, cw, alog_x[0:1], dtb_x[0:1], None, None, None,
                                    reverse=False, with_ctx_out=with_ctx_out)
    return _deltanet_direction(zb, zbc, cw, alog_x[1:2], dtb_x[1:2], g, o_f, oc_f,
                               reverse=True, with_ctx_out=with_ctx_out)


def _ret_masks():
    lane_q = _lane_iota(128)
    lane_v = _lane_iota(256)
    rq = (lax.broadcasted_iota(jnp.int32, (128, 256), 0) >> 4) & 3
    cv = lax.broadcasted_iota(jnp.int32, (128, 256), 1) >> 6
    return (lane_q >> 4) & 3, lane_v >> 6, rq == cv


def _ret_ctx_state(kc, vc, lgq, bd, *, reverse):
    l = kc.shape[0]
    pos = lax.broadcasted_iota(jnp.int32, (l, 128), 0).astype(F32)
    wgt = pos if reverse else (l - 1.0 - pos)
    return jnp.where(bd, _dot_tn(kc * jnp.exp(wgt * lgq), vc), 0.0)


def _ret_state_update(r_old, k, v, lgq, lgv, bd, *, reverse):
    c = k.shape[0]
    pos = lax.broadcasted_iota(jnp.int32, (c, 128), 0).astype(F32)
    wgt = pos if reverse else (c - 1.0 - pos)
    return r_old * jnp.exp(c * lgv) + jnp.where(bd, _dot_tn(k * jnp.exp(wgt * lgq), v), 0.0)


def _ret_state_kernel(*refs, has_ctx):
    if has_ctx:
        k_ref, v_ref, kc_ref, vc_ref, l1q_ref, l1v_ref, o_ref, r_s = refs
    else:
        k_ref, v_ref, l1q_ref, l1v_ref, o_ref, r_s = refs
    _, _, bd = _ret_masks()
    lgq = jnp.log1p(-jnp.exp(l1q_ref[1:2, :]))
    lgv = jnp.log1p(-jnp.exp(l1v_ref[1:2, :]))

    @pl.when(pl.program_id(1) == 0)
    def _():
        if has_ctx:
            r_s[...] = _ret_ctx_state(kc_ref[0], vc_ref[0], lgq, bd, reverse=True)
        else:
            r_s[...] = jnp.zeros_like(r_s)

    r_old = r_s[...]
    o_ref[0, 0] = r_old
    r_s[...] = _ret_state_update(r_old, k_ref[0], v_ref[0], lgq, lgv, bd, reverse=True)


def _ret_main_kernel(*refs, has_ctx):
    if has_ctx:
        q_ref, k_ref, v_ref, g_ref, rb_ref, kc_ref, vc_ref, l1q_ref, l1v_ref, l1r_ref, ng_ref, o_ref, r_s = refs
    else:
        q_ref, k_ref, v_ref, g_ref, rb_ref, l1q_ref, l1v_ref, l1r_ref, ng_ref, o_ref, r_s = refs
    hq, hv, bd = _ret_masks()
    lgq = jnp.log1p(-jnp.exp(l1q_ref[...]))
    lgv = jnp.log1p(-jnp.exp(l1v_ref[...]))
    lgr = jnp.log1p(-jnp.exp(l1r_ref[...]))

    @pl.when(pl.program_id(1) == 0)
    def _():
        if has_ctx:
            r_s[...] = _ret_ctx_state(kc_ref[0], vc_ref[0], lgq[0:1], bd, reverse=False)
        else:
            r_s[...] = jnp.zeros_like(r_s)

    q = q_ref[0] * (RET_QK_DIM ** -0.5)
    k = k_ref[0]
    v = v_ref[0]
    c = q.shape[0]
    ii = lax.broadcasted_iota(jnp.int32, (c, c), 0)
    jj = lax.broadcasted_iota(jnp.int32, (c, c), 1)
    rel = (ii - jj).astype(F32)
    o = None
    for h in range(RET_HEADS):
        sc = _dot_nt(jnp.where(hq == h, q, 0.0), k)
        d_f = jnp.where(ii >= jj, jnp.exp(jnp.maximum(rel, 0.0) * lgr[h:h + 1, :]), 0.0)
        d_b = jnp.where(jj >= ii, jnp.exp(jnp.maximum(-rel, 0.0) * lgr[4 + h:5 + h, :]), 0.0)
        t = jnp.where(hv == h, _dot(sc * (d_f + d_b), v), 0.0)
        o = t if o is None else o + t
    pos = lax.broadcasted_iota(jnp.int32, (c, 128), 0).astype(F32)
    r_old = r_s[...]
    o = o + _dot(q * jnp.exp((pos + 1.0) * lgq[0:1]), r_old) + _dot(q * jnp.exp((c - pos) * lgq[1:2]), rb_ref[0, 0])
    r_s[...] = _ret_state_update(r_old, k, v, lgq[0:1], lgv[0:1], bd, reverse=False)
    ones64 = _group_ones(256, 6)
    mu = _group_sum(o, ones64) * (1.0 / RET_V_DIM)
    d = o - mu
    var = _group_sum(d * d, ones64) * (1.0 / RET_V_DIM)
    o_ref[0] = d * lax.rsqrt(var + NORM_EPS) * ng_ref[...] * _silu(g_ref[0])


def _retention(zc, zcc, log1m_gamma, norm_g):
    b, t, _ = zc.shape
    c = min(RET_TILE, t)
    assert t % c == 0
    nc = t // c
    has_ctx = zcc is not None
    l1q = jnp.tile(jnp.repeat(log1m_gamma, 16, axis=-1), (1, 2))
    l1v = jnp.repeat(log1m_gamma, 64, axis=-1)
    l1r = jnp.broadcast_to(log1m_gamma.reshape(8, 1), (8, c))
    col = lambda cb, w_, f: pl.BlockSpec((1, c, w_), lambda i, n: (i, f(n), cb))
    fwd = lambda n: n
    bwd = lambda n: nc - 1 - n
    ctx_specs, ctx_args = [], []
    if has_ctx:
        l = zcc.shape[1]
        ctx_specs = [pl.BlockSpec((1, l, 128), lambda i, n: (i, 0, 1)), pl.BlockSpec((1, l, 256), lambda i, n: (i, 0, 1))]
        ctx_args = [zcc, zcc]
    rb = pl.pallas_call(
        functools.partial(_ret_state_kernel, has_ctx=has_ctx),
        name="retention_state",
        grid=(b, nc),
        in_specs=[col(1, 128, bwd), col(1, 256, bwd)] + ctx_specs + [_const_spec((2, 128)), _const_spec((2, 256))],
        out_specs=pl.BlockSpec((1, 1, 128, 256), lambda i, n: (i, bwd(n), 0, 0)),
        out_shape=jax.ShapeDtypeStruct((b, nc, 128, 256), F32),
        scratch_shapes=[pltpu.VMEM((128, 256), F32)],
        compiler_params=_params("parallel", "arbitrary"),
    )(zc, zc, *ctx_args, l1q, l1v)
    return pl.pallas_call(
        functools.partial(_ret_main_kernel, has_ctx=has_ctx),
        name="retention_main",
        grid=(b, nc),
        in_specs=[col(0, 128, fwd), col(1, 128, fwd), col(1, 256, fwd), col(2, 256, fwd),
                  pl.BlockSpec((1, 1, 128, 256), lambda i, n: (i, n, 0, 0))] + ctx_specs
                 + [_const_spec((2, 128)), _const_spec((2, 256)), _const_spec((8, c)), _const_spec((1, 256))],
        out_specs=pl.BlockSpec((1, c, 256), lambda i, n: (i, n, 0)),
        out_shape=jax.ShapeDtypeStruct((b, t, 256), F32),
        scratch_shapes=[pltpu.VMEM((128, 256), F32)],
        compiler_params=_params("parallel", "arbitrary"),
    )(zc, zc, zc, zc, rb, *ctx_args, l1q, l1v, l1r, norm_g.reshape(1, 256))


def _mla_weights(w_uq, w_ukv):
    dq = MLA_NOPE_DIM + MLA_ROPE_DIM
    qa = np.full((MLA_HEADS, 128), -1, np.int32)
    qb = np.full((MLA_HEADS, 128), -1, np.int32)
    kk = np.full((MLA_HEADS, 128), -1, np.int32)
    vv = np.zeros((MLA_HEADS, 64), np.int32)
    for h in range(MLA_HEADS):
        qa[h, 0:96] = h * dq + np.arange(96)
        qb[h, 64:80] = h * dq + 64 + 16 + np.arange(16)
        qb[h, 80:96] = h * dq + 64 + np.arange(16)
        kk[h, 0:64] = h * 128 + np.arange(64)
        vv[h] = h * 128 + 64 + np.arange(64)

    def take(w, idx):
        idx = idx.reshape(-1)
        out = jnp.take(w, jnp.asarray(np.maximum(idx, 0)), axis=1) * jnp.asarray((idx >= 0).astype(np.float32))[None, :]
        return out.astype(BF16)

    return take(w_uq, qa), take(w_uq, qb), take(w_ukv, kk), take(w_ukv, vv)


def _mla_prep_kernel(*refs, rope):
    if rope:
        zd_ref, qn_ref, kn_ref, wqa_ref, wqb_ref, wk_ref, wv_ref, cd_ref, sd_ref, q_ref, k_ref, v_ref = refs
    else:
        zd_ref, qn_ref, kn_ref, wqa_ref, wk_ref, wv_ref, q_ref, k_ref, v_ref = refs
    zd = zd_ref[0]
    cq, ckv, kr = zd[:, 0:256], zd[:, 256:384], zd[:, 384:512]
    nq = cq * lax.rsqrt(jnp.mean(cq * cq, axis=-1, keepdims=True) + NORM_EPS) * qn_ref[...]
    nkv = ckv * lax.rsqrt(jnp.mean(ckv * ckv, axis=-1, keepdims=True) + NORM_EPS) * kn_ref[...]
    scale = (MLA_NOPE_DIM + MLA_ROPE_DIM) ** -0.5
    qa = _dot(nq, wqa_ref[...])
    if rope:
        qb = _dot(nq, wqb_ref[...])
        cd, sd = cd_ref[...], sd_ref[...]
    kall = _dot(nkv, wk_ref[...])
    for h in range(MLA_HEADS):
        sl = slice(h * 128, (h + 1) * 128)
        qh = qa[:, sl]
        if rope:
            qh = qh * cd + qb[:, sl] * sd
        q_ref[0, :, sl] = (qh * scale).astype(BF16)
        k_ref[0, :, sl] = (kall[:, sl] + kr).astype(BF16)
    v_ref[0] = _dot(nkv, wv_ref[...]).astype(BF16)


def _mla_prep(zd, q_norm, kv_norm, weights, tables):
    b, t, _ = zd.shape
    tm = min(TOK_TILE, t)
    rope = tables is not None
    wqa, wqb, wk, wv = weights
    tok = lambda w_: pl.BlockSpec((1, tm, w_), lambda i, j: (i, j, 0))
    tab = pl.BlockSpec((tm, 128), lambda i, j: (j, 0))
    in_specs = [tok(ZD_W), _const_spec((1, 256)), _const_spec((1, 128)), _const_spec((256, 512))]
    args = [zd, q_norm.reshape(1, 256), kv_norm.reshape(1, 128), wqa]
    if rope:
        in_specs.append(_const_spec((256, 512)))
        args.append(wqb)
    in_specs += [_const_spec((128, 512)), _const_spec((128, 256))]
    args += [wk, wv]
    if rope:
        in_specs += [tab, tab]
        args += list(tables)
    return pl.pallas_call(
        functools.partial(_mla_prep_kernel, rope=rope),
        name="mla_prep",
        grid=(b, t // tm),
        in_specs=in_specs,
        out_specs=[tok(512), tok(512), tok(256)],
        out_shape=[jax.ShapeDtypeStruct((b, t, 512), BF16), jax.ShapeDtypeStruct((b, t, 512), BF16),
                   jax.ShapeDtypeStruct((b, t, 256), BF16)],
        compiler_params=_params("parallel", "parallel"),
    )(*args)


def _mla_attn_kernel(*refs, two_segments):
    if two_segments:
        q_ref, k1_ref, v1_ref, k2_ref, v2_ref, o_ref = refs
    else:
        q_ref, k2_ref, v2_ref, o_ref = refs
    lane = _lane_iota(256)
    acc = None
    for h in range(MLA_HEADS):
        sl = slice(h * 128, (h + 1) * 128)
        qh = q_ref[0, :, sl]
        s2 = lax.dot_general(qh, k2_ref[0, :, sl], (((1,), (1,)), ((), ())), preferred_element_type=F32)
        m = jnp.max(s2, axis=-1, keepdims=True)
        if two_segments:
            s1 = lax.dot_general(qh, k1_ref[0, :, sl], (((1,), (1,)), ((), ())), preferred_element_type=F32)
            m = jnp.maximum(m, jnp.max(s1, axis=-1, keepdims=True))
        p2 = jnp.exp(s2 - m)
        den = jnp.sum(p2, axis=-1, keepdims=True)
        o = jnp.dot(p2.astype(BF16), v2_ref[0], preferred_element_type=F32)
        if two_segments:
            p1 = jnp.exp(s1 - m)
            den = den + jnp.sum(p1, axis=-1, keepdims=True)
            o = o + jnp.dot(p1.astype(BF16), v1_ref[0], preferred_element_type=F32)
        o = o * jnp.where((lane >> 6) == h, 1.0 / den, 0.0)
        acc = o if acc is None else acc + o
    o_ref[0] = acc


def _mla_attend(q, kv_latent, kv_ctx):
    b, t, _ = q.shape
    tq = min(MLA_TQ, t)
    two = kv_latent is not None
    full = lambda n, w_: pl.BlockSpec((1, n, w_), lambda i, j: (i, 0, 0))
    in_specs = [pl.BlockSpec((1, tq, 512), lambda i, j: (i, j, 0))]
    args = [q]
    for kv in ((kv_latent, kv_ctx) if two else (kv_ctx,)):
        n = kv[0].shape[1]
        in_specs += [full(n, 512), full(n, 256)]
        args += list(kv)
    return pl.pallas_call(
        functools.partial(_mla_attn_kernel, two_segments=two),
        name="mla_attention",
        grid=(b, t // tq),
        in_specs=in_specs,
        out_specs=pl.BlockSpec((1, tq, 256), lambda i, j: (i, j, 0)),
        out_shape=jax.ShapeDtypeStruct((b, t, 256), F32),
        compiler_params=_params("parallel", "arbitrary"),
    )(*args)


def _out_proj_kernel(ya_ref, yb_ref, yc_ref, yd_ref, x_ref, g_ref, w_ref, lg_ref, lb_ref, o_ref):
    y = jnp.concatenate([ya_ref[0], yb_ref[0], yc_ref[0], yd_ref[0]], axis=-1)
    proj = _dot(y, w_ref[...])
    o_ref[0] = _layer_norm(DEEPNORM_ALPHA * x_ref[0] + g_ref[0] * proj, lg_ref[...], lb_ref[...])


def _out_proj(ys, x, gate, w, ln_g, ln_b):
    b, t, d = x.shape
    tm = min(TOK_TILE, t)
    tok = lambda w_: pl.BlockSpec((1, tm, w_), lambda i, j: (i, j, 0))
    vec = pl.BlockSpec((1, 1, d), lambda i, j: (i, 0, 0))
    return pl.pallas_call(
        _out_proj_kernel,
        name="out_proj_ln",
        grid=(b, t // tm),
        in_specs=[tok(256)] * 4 + [tok(d), vec, _const_spec((d, d)), _const_spec((1, d)), _const_spec((1, d))],
        out_specs=tok(d),
        out_shape=jax.ShapeDtypeStruct((b, t, d), F32),
        compiler_params=_params("parallel", "parallel"),
    )(*ys, x, gate, w, ln_g.reshape(1, d), ln_b.reshape(1, d))


def _mlp_kernel(x_ref, sc_ref, sh_ref, g_ref, w1_ref, w2_ref, lg_ref, lb_ref, o_ref):
    x = x_ref[0]
    h = (x * (1.0 + sc_ref[0]) + sh_ref[0]).astype(BF16)
    acc = None
    step = 1024
    for j in range(D_FF // step):
        u = jnp.maximum(jnp.dot(h, w1_ref[:, j * step:(j + 1) * step], preferred_element_type=F32), 0.0)
        t = jnp.dot((u * u).astype(BF16), w2_ref[j * step:(j + 1) * step, :], preferred_element_type=F32)
        acc = t if acc is None else acc + t
    o_ref[0] = _layer_norm(DEEPNORM_ALPHA * x + g_ref[0] * acc, lg_ref[...], lb_ref[...])


def _mlp(x, sc, sh, gate, w1, w2, ln_g, ln_b):
    b, t, d = x.shape
    tm = min(TOK_TILE, t)
    tok = pl.BlockSpec((1, tm, d), lambda i, j: (i, j, 0))
    vec = pl.BlockSpec((1, 1, d), lambda i, j: (i, 0, 0))
    return pl.pallas_call(
        _mlp_kernel,
        name="mlp_ln",
        grid=(b, t // tm),
        in_specs=[tok, vec, vec, vec, _const_spec((d, D_FF)), _const_spec((D_FF, d)), _const_spec((1, d)),
                  _const_spec((1, d))],
        out_specs=tok,
        out_shape=jax.ShapeDtypeStruct((b, t, d), F32),
        compiler_params=_params("parallel", "parallel"),
    )(x, sc, sh, gate, w1, w2, ln_g.reshape(1, d), ln_b.reshape(1, d))


def _rope_tables(n_tok):
    rows = n_tok // GRID_W
    freqs = lambda dim: ROPE_BASE ** (-jnp.arange(0, dim, 2, dtype=F32) / dim)
    row = jnp.broadcast_to(jnp.arange(rows, dtype=F32)[:, None], (rows, GRID_W)).reshape(-1)
    col = jnp.broadcast_to(jnp.arange(GRID_W, dtype=F32)[None, :], (rows, GRID_W)).reshape(-1)

    def axial(rot_dim):
        inv = freqs(rot_dim // 2)
        return jnp.concatenate([row[:, None] * inv, col[:, None] * inv], axis=-1)

    ang_a = axial(SWA_HEAD_DIM)
    ang_c = jnp.arange(n_tok, dtype=F32)[:, None] * freqs(RET_QK_DIM)
    ang_d = axial(MLA_ROPE_DIM)
    pair = lambda a, reps: (jnp.tile(jnp.cos(a), (1, 2 * reps)),
                            jnp.concatenate([-jnp.tile(jnp.sin(a), (1, reps)), jnp.tile(jnp.sin(a), (1, reps))], axis=-1))
    ca, sa = pair(ang_a, 2)
    cc, sc = pair(ang_c, 4)
    one = jnp.ones((n_tok, 64), F32)
    zero = jnp.zeros((n_tok, 64), F32)
    cd = jnp.concatenate([one, jnp.cos(ang_d), jnp.cos(ang_d), one[:, :32]], axis=-1)
    sd = jnp.concatenate([zero, -jnp.sin(ang_d), jnp.sin(ang_d), zero[:, :32]], axis=-1)
    return ca, sa, cc, sc, cd, sd


def kernel(x, c, ctx, c_ctx, ada_w, ada_b, w_in, swa_sink, dn_conv_w, dn_a_log, dn_dt_bias, dn_norm_g,
           ret_log1m_gamma, ret_norm_g, mla_q_norm, mla_w_uq, mla_kv_norm, mla_w_ukv, w_out, ln1_g, ln1_b,
           w_ff1, w_ff2, ln2_g, ln2_b):
    b, n_tok, d = x.shape
    tables = _rope_tables(n_tok)
    cc = jnp.concatenate([c, c_ctx[None, :], jnp.zeros((7, d), F32)], axis=0)
    perm_a = np.concatenate([np.arange(64) + 64 * hd for hd in (0, 2, 1, 3)])
    out_rows = np.concatenate([perm_a, np.arange(256, 1024)])
    xc = ctx
    for layer in range(DEPTH):
        with_ctx_out = layer < DEPTH - 1
        mod = _ada_modulation(cc, ada_w[layer], ada_b[layer])
        sh1, sc1, g1, sh2, sc2, g2 = jnp.split(mod[:b, None, :], 6, axis=-1)
        cmod = jnp.broadcast_to(mod[b][None, None, :], (b, 1, 6 * d))
        csh1, csc1, cg1, csh2, csc2, cg2 = jnp.split(cmod, 6, axis=-1)
        w_in_p = _permute_in_proj(w_in[layer])
        za, zb, zc, zd = _in_proj(x, sc1, sh1, w_in_p, tables)
        zac, zbc, zcc, zdc = _in_proj(xc, csc1, csh1, w_in_p, None)
        sink = swa_sink[layer]
        ya = _swa_local(za, zac, sink)
        yb, ybc = _deltanet(zb, zbc, dn_conv_w[layer], dn_a_log[layer], dn_dt_bias[layer], dn_norm_g[layer],
                            with_ctx_out)
        yr = _retention(zc, zcc, ret_log1m_gamma[layer], ret_norm_g[layer])
        mla_w = _mla_weights(mla_w_uq[layer], mla_w_ukv[layer])
        qd, kd, vd = _mla_prep(zd, mla_q_norm[layer], mla_kv_norm[layer], mla_w, tables[4:6])
        qdc, kdc, vdc = _mla_prep(zdc, mla_q_norm[layer], mla_kv_norm[layer], mla_w, None)
        yd = _mla_attend(qd, (kd, vd), (kdc, vdc))
        w_out_p = jnp.take(w_out[layer], jnp.asarray(out_rows), axis=0).astype(BF16)
        w1 = w_ff1[layer].astype(BF16)
        w2 = w_ff2[layer].astype(BF16)
        x_new = _out_proj((ya, yb, yr, yd), x, g1, w_out_p, ln1_g[layer], ln1_b[layer])
        x_new = _mlp(x_new, sc2, sh2, g2, w1, w2, ln2_g[layer], ln2_b[layer])
        if with_ctx_out:
            yac = _swa_ctx(zac, sink)
            yrc = _retention(zcc, None, ret_log1m_gamma[layer], ret_norm_g[layer])
            ydc = _mla_attend(qdc, None, (kdc, vdc))
            xc_new = _out_proj((yac, ybc, yrc, ydc), xc, cg1, w_out_p, ln1_g[layer], ln1_b[layer])
            xc = _mlp(xc_new, csc2, csh2, cg2, w1, w2, ln2_g[layer], ln2_b[layer])
        x = x_new
    return x
```

```python
import functools
import math

import numpy as np
import jax
import jax.numpy as jnp
from jax import lax
from jax.experimental import pallas as pl
from jax.experimental.pallas import tpu as pltpu

F32 = jnp.float32
BF16 = jnp.bfloat16

D_MODEL = 1024
DEPTH = 2
GRID_W = 64
SWA_HEADS, SWA_KV_HEADS, SWA_HEAD_DIM, SWA_WINDOW = 4, 2, 64, 128
DN_HEADS, DN_HEAD_DIM, DN_CONV, DN_CHUNK = 4, 64, 5, 64
RET_HEADS, RET_QK_DIM, RET_V_DIM = 4, 32, 64
MLA_HEADS, MLA_Q_RANK, MLA_KV_RANK, MLA_NOPE_DIM, MLA_ROPE_DIM, MLA_V_DIM = 4, 256, 128, 64, 32, 64
D_FF = 4 * D_MODEL
ROPE_BASE = 10000.0
NORM_EPS = 1e-6
LN_EPS = 1e-5
DEEPNORM_ALPHA = (2 * DEPTH) ** 0.25
IN_SPLITS = (256, 128, 128, 768, 256, 16, 128, 128, 256, 256, 256, 128, 32)

ZA_W, ZB_W, ZC_W, ZD_W, KRS_W = 512, 1152, 768, 512, 128
Z_OFF = np.cumsum((0, ZA_W, ZB_W, ZC_W, ZD_W, KRS_W))
Z_W = int(Z_OFF[-1])

VMEM_LIMIT = 48 * 1024 * 1024
NEG_BIG = -1e30

DN_TILE = 256
RET_TILE = 256
MLA_TQ = 256
MLA_KEY_CHUNK = 512
LOG2_E = 1.4426950408889634
TOK_TILE = 256


def _dot(a, b):
    return jnp.dot(a.astype(BF16), b.astype(BF16), preferred_element_type=F32)


def _dot_nt(a, b):
    return lax.dot_general(a.astype(BF16), b.astype(BF16), (((1,), (1,)), ((), ())), preferred_element_type=F32)


def _dot_tn(a, b):
    return lax.dot_general(a.astype(BF16), b.astype(BF16), (((0,), (0,)), ((), ())), preferred_element_type=F32)


def _split2(a):
    hi = a.astype(BF16)
    lo = (a - hi.astype(F32)).astype(BF16)
    return hi, lo


def _split3(a):
    hi = a.astype(BF16)
    r = a - hi.astype(F32)
    mid = r.astype(BF16)
    lo = (r - mid.astype(F32)).astype(BF16)
    return hi, mid, lo


def _dot_sel(a, sel):
    return sum(jnp.dot(p, sel, preferred_element_type=F32) for p in _split3(a))


def _dot_sel_nt(sel, a):
    dn = (((1,), (1,)), ((), ()))
    return sum(lax.dot_general(sel, p, dn, preferred_element_type=F32) for p in _split3(a))


def _dot_hp(a, b):
    ah, al = _split2(a)
    bh, bl = _split2(b)
    f = lambda x, y: jnp.dot(x, y, preferred_element_type=F32)
    return f(ah, bh) + (f(ah, bl) + f(al, bh))


def _group_ones(width, shift):
    r = lax.broadcasted_iota(jnp.int32, (width, width), 0) >> shift
    c = lax.broadcasted_iota(jnp.int32, (width, width), 1) >> shift
    return jnp.where(r == c, 1.0, 0.0).astype(BF16)


def _group_sum(x, ones_bd):
    hi, lo = _split2(x)
    return jnp.dot(hi, ones_bd, preferred_element_type=F32) + jnp.dot(lo, ones_bd, preferred_element_type=F32)


def _silu(x):
    return x * jax.nn.sigmoid(x)


def _layer_norm(v, g, b):
    mu = jnp.mean(v, axis=-1, keepdims=True)
    d = v - mu
    var = jnp.mean(d * d, axis=-1, keepdims=True)
    return d * lax.rsqrt(var + LN_EPS) * g + b


def _lane_iota(width):
    return lax.broadcasted_iota(jnp.int32, (1, width), 1)


def _params(*sem):
    return pltpu.CompilerParams(dimension_semantics=sem, vmem_limit_bytes=VMEM_LIMIT)


def _const_spec(shape):
    nd = len(shape)
    return pl.BlockSpec(shape, lambda *_: (0,) * nd)


def _ada_kernel(c_ref, w_ref, b_ref, o_ref):
    o_ref[...] = _dot(_silu(c_ref[...]), w_ref[...]) + b_ref[...]


def _ada_modulation(cc, w, b):
    rows, d = cc.shape
    n = w.shape[1]
    tn = 1536
    return pl.pallas_call(
        _ada_kernel,
        name="ada_modulation",
        grid=(n // tn,),
        in_specs=[pl.BlockSpec((rows, d), lambda j: (0, 0)), pl.BlockSpec((d, tn), lambda j: (0, j)),
                  pl.BlockSpec((1, tn), lambda j: (0, j))],
        out_specs=pl.BlockSpec((rows, tn), lambda j: (0, j)),
        out_shape=jax.ShapeDtypeStruct((rows, n), F32),
        compiler_params=_params("parallel"),
    )(cc, w, b.reshape(1, n))


def _in_proj_columns():
    off = np.cumsum((0,) + IN_SPLITS)
    aq, ak, av, bqkv, bz, bab, cq, ck, cv, cg, dcq, dckv, dkr = (int(o) for o in off[:-1])
    ar = lambda base, n: list(range(base, base + n))
    cols = []
    for g in range(2):
        for half in range(2):
            for h in range(2):
                cols += ar(aq + (2 * h + g) * 64 + half * 32, 32)
    for half in range(2):
        for h in range(2):
            cols += ar(ak + h * 64 + half * 32, 32)
    cols += ar(av, 128)
    cols += ar(bqkv, 768) + ar(bz, 256) + ar(bab, 16) + [-1] * 112
    for base in (cq, ck):
        for half in range(2):
            for h in range(4):
                cols += ar(base + h * 32 + half * 16, 16)
    cols += ar(cv, 256) + ar(cg, 256)
    cols += ar(dcq, 256) + ar(dckv, 128)
    cols += [-1] * 64 + ar(dkr, 32) + [-1] * 32
    cols += [-1] * 64 + ar(dkr + 16, 16) + ar(dkr, 16) + [-1] * 32
    cols = np.asarray(cols, np.int32)
    assert cols.shape[0] == Z_W
    return cols


_IN_COLS = _in_proj_columns()


def _permute_in_proj(w_in):
    idx = jnp.asarray(np.maximum(_IN_COLS, 0))
    keep = jnp.asarray((_IN_COLS >= 0).astype(np.float32))
    return (jnp.take(w_in, idx, axis=1) * keep[None, :]).astype(BF16)


def _in_proj_kernel(*refs, rope):
    if rope:
        x_ref, sc_ref, sh_ref, w_ref, ca_ref, sa_ref, cc_ref, sc2_ref, cd_ref, sd_ref, oa, ob, oc, od = refs
    else:
        x_ref, sc_ref, sh_ref, w_ref, oa, ob, oc, od = refs
    h = x_ref[0] * (1.0 + sc_ref[0]) + sh_ref[0]
    z = _dot(h, w_ref[...])
    za = z[:, Z_OFF[0]:Z_OFF[1]]
    zc = z[:, Z_OFF[2]:Z_OFF[3]]
    zd = z[:, Z_OFF[3]:Z_OFF[4]]
    ob[0] = z[:, Z_OFF[1]:Z_OFF[2]]
    if not rope:
        oa[0] = za
        oc[0] = zc
        od[0] = zd
        return
    krs = z[:, Z_OFF[4]:Z_OFF[5]]

    def rot(t, c, s):
        return t * c + pltpu.roll(t, 64, 1) * s

    ca, sa = ca_ref[...], sa_ref[...]
    for j in range(3):
        oa[0, :, j * 128:(j + 1) * 128] = rot(za[:, j * 128:(j + 1) * 128], ca, sa)
    oa[0, :, 384:512] = za[:, 384:512]
    cc, sc2 = cc_ref[...], sc2_ref[...]
    for j in range(2):
        oc[0, :, j * 128:(j + 1) * 128] = rot(zc[:, j * 128:(j + 1) * 128], cc, sc2)
    oc[0, :, 256:768] = zc[:, 256:768]
    od[0, :, 0:384] = zd[:, 0:384]
    od[0, :, 384:512] = zd[:, 384:512] * cd_ref[...] + krs * sd_ref[...]


def _in_proj(x, sc, sh, w, tables):
    b, t, d = x.shape
    tm = min(TOK_TILE, t)
    rope = tables is not None
    tok = lambda w_: pl.BlockSpec((1, tm, w_), lambda i, j: (i, j, 0))
    vec = pl.BlockSpec((1, 1, d), lambda i, j: (i, 0, 0))
    tab = pl.BlockSpec((tm, 128), lambda i, j: (j, 0))
    in_specs = [tok(d), vec, vec, _const_spec((d, Z_W))] + ([tab] * 6 if rope else [])
    args = [x, sc, sh, w] + (list(tables) if rope else [])
    widths = (ZA_W, ZB_W, ZC_W, ZD_W)
    return pl.pallas_call(
        functools.partial(_in_proj_kernel, rope=rope),
        name="in_proj_rope" if rope else "in_proj_ctx",
        grid=(b, t // tm),
        in_specs=in_specs,
        out_specs=[tok(w_) for w_ in widths],
        out_shape=[jax.ShapeDtypeStruct((b, t, w_), F32) for w_ in widths],
        compiler_params=_params("parallel", "parallel"),
    )(*args)


def _swa_heads(sink_ref, qg, keys, vals, mask):
    lane = _lane_iota(128)
    scale = SWA_HEAD_DIM ** -0.5
    kb = keys.astype(BF16)
    vb = vals.astype(BF16)
    hg = [(h, g) for g in range(2) for h in range(2)]
    s = [_dot_nt(jnp.where(((lane >> 5) & 1) == h, qg[g] * scale, 0.0), kb) for h, g in hg]
    if mask is not None:
        s = [jnp.where(mask, x, NEG_BIG) for x in s]
    snk = [sink_ref[2 * h + g] for h, g in hg]
    m = [jnp.maximum(jnp.max(x, axis=-1, keepdims=True), k_) for x, k_ in zip(s, snk)]
    p = [jnp.exp(x - y) for x, y in zip(s, m)]
    den = [jnp.sum(x, axis=-1, keepdims=True) + jnp.exp(k_ - y) for x, k_, y in zip(p, snk, m)]
    o = [jnp.dot(x.astype(BF16), vb, preferred_element_type=F32) for x in p]
    o = [x * jnp.where((lane >> 6) == h, 1.0 / d, 0.0) for x, d, (h, g) in zip(o, den, hg)]
    return jnp.concatenate([o[0] + o[1], o[2] + o[3]], axis=-1)


def _swa_local_kernel(sink_ref, q_ref, kp_ref, k0_ref, kn_ref, vp_ref, v0_ref, vn_ref, kc_ref, vc_ref, o_ref, *, nb):
    n = pl.program_id(1)
    w = SWA_WINDOW
    keys = jnp.concatenate([kp_ref[0], k0_ref[0], kn_ref[0], kc_ref[0]], axis=0)
    vals = jnp.concatenate([vp_ref[0], v0_ref[0], vn_ref[0], vc_ref[0]], axis=0)
    nk = keys.shape[0]
    i = lax.broadcasted_iota(jnp.int32, (w, nk), 0)
    j = lax.broadcasted_iota(jnp.int32, (w, nk), 1)
    lo = jnp.where(n > 0, 0, w)
    hi = jnp.where(n < nb - 1, 3 * w, 2 * w)
    band = (j >= i) & (j <= i + 2 * w) & (j >= lo) & (j < hi)
    mask = band | (j >= 3 * w)
    q = q_ref[0]
    o_ref[0] = _swa_heads(sink_ref, (q[:, :128], q[:, 128:]), keys, vals, mask)


def _swa_ctx_kernel(sink_ref, q_ref, kc_ref, vc_ref, o_ref):
    q = q_ref[0]
    o_ref[0] = _swa_heads(sink_ref, (q[:, :128], q[:, 128:]), kc_ref[0], vc_ref[0], None)


_SMEM_SPEC = pl.BlockSpec(memory_space=pltpu.SMEM)


def _swa_local(za, zac, sink):
    b, s, _ = za.shape
    l = zac.shape[1]
    w = SWA_WINDOW
    nb = s // w
    blk = lambda col, f: pl.BlockSpec((1, w, 128), lambda i, n: (i, f(n), col))
    prev = lambda n: jnp.maximum(n - 1, 0)
    same = lambda n: n
    nxt = lambda n: jnp.minimum(n + 1, nb - 1)
    ctx = lambda col: pl.BlockSpec((1, l, 128), lambda i, n: (i, 0, col))
    return pl.pallas_call(
        functools.partial(_swa_local_kernel, nb=nb),
        name="swa_local",
        grid=(b, nb),
        in_specs=[_SMEM_SPEC, pl.BlockSpec((1, w, 256), lambda i, n: (i, n, 0)),
                  blk(2, prev), blk(2, same), blk(2, nxt), blk(3, prev), blk(3, same), blk(3, nxt), ctx(2), ctx(3)],
        out_specs=pl.BlockSpec((1, w, 256), lambda i, n: (i, n, 0)),
        out_shape=jax.ShapeDtypeStruct((b, s, 256), F32),
        compiler_params=_params("parallel", "parallel"),
    )(sink, za, za, za, za, za, za, za, zac, zac)


def _swa_ctx(zac, sink):
    b, l, _ = zac.shape
    ctx = lambda col, w_: pl.BlockSpec((1, l, w_), lambda i: (i, 0, col))
    return pl.pallas_call(
        _swa_ctx_kernel,
        name="swa_ctx",
        grid=(b,),
        in_specs=[_SMEM_SPEC, ctx(0, 256), ctx(2, 128), ctx(3, 128)],
        out_specs=pl.BlockSpec((1, l, 256), lambda i: (i, 0, 0)),
        out_shape=jax.ShapeDtypeStruct((b, l, 256), F32),
        compiler_params=_params("parallel"),
    )(sink, zac, zac, zac)


def _dn_prepare(x, prev8, next8, cw_ref, alog, dtb, xext, *, direction):
    t = x.shape[0]
    xext[0:8, :] = prev8
    xext[8:8 + t, :] = x[:, 0:768]
    xext[8 + t:16 + t, :] = next8
    pad = DN_CONV // 2
    y = None
    for kk in range(DN_CONV):
        term = cw_ref[kk:kk + 1, :] * xext[pl.ds(8 - pad + kk, t), :]
        y = term if y is None else y + term
    y = _silu(y)
    ones64 = _group_ones(256, 6)
    q, k, v = y[:, 0:256], y[:, 256:512], y[:, 512:768]
    q = q * lax.rsqrt(_group_sum(q * q, ones64) + NORM_EPS) * (DN_HEAD_DIM ** -0.5)
    k = k * lax.rsqrt(_group_sum(k * k, ones64) + NORM_EPS)
    ab = x[:, 1024:1152]
    r = lax.broadcasted_iota(jnp.int32, (128, 256), 0)
    hd = lax.broadcasted_iota(jnp.int32, (128, 256), 1) >> 6
    sel_g = jnp.where(r == direction * 8 + hd, 1.0, 0.0).astype(BF16)
    sel_b = jnp.where(r == direction * 8 + 4 + hd, 1.0, 0.0).astype(BF16)
    graw = _dot_sel(ab, sel_g) + dtb
    softplus = jnp.maximum(graw, 0.0) + jnp.log1p(jnp.exp(-jnp.abs(graw)))
    lg = -jnp.exp(alog) * softplus
    beta = jax.nn.sigmoid(_dot_sel(ab, sel_b))
    return q, k, v, lg, beta


def _block_triangular_inverse(mats, ii, jj):
    same = lambda shift: (ii >> shift) == (jj >> shift)
    mm = lambda x, y: jnp.dot(x, y, preferred_element_type=F32)
    eye = jnp.where(ii == jj, 1.0, 0.0)
    a8 = [jnp.where(same(3), a, 0.0) for a in mats]
    a8b = [x.astype(BF16) for x in a8]
    p2 = [mm(x, x).astype(BF16) for x in a8b]
    t = [eye - x for x in a8]
    t = [x + mm(x.astype(BF16), p) for x, p in zip(t, p2)]
    p4 = [mm(p, p).astype(BF16) for p in p2]
    t = [x + mm(x.astype(BF16), p) for x, p in zip(t, p4)]
    for shift in (4, 5, 6):
        sel = same(shift) & jnp.logical_not(same(shift - 1))
        off = [jnp.where(sel, a, 0.0).astype(BF16) for a in mats]
        tb = [x.astype(BF16) for x in t]
        to = [mm(x, o).astype(BF16) for x, o in zip(tb, off)]
        t = [x - mm(y, xb) for x, y, xb in zip(t, to, tb)]
    return t


def _dn_tile(q, k, v, lg, beta, st, *, reverse, emit):
    tt = q.shape[0]
    c = DN_CHUNK
    nchunk = tt // c
    ii = lax.broadcasted_iota(jnp.int32, (tt, tt), 0)
    jj = lax.broadcasted_iota(jnp.int32, (tt, tt), 1)
    chunk = (ii >> 6) == (jj >> 6)
    incl = chunk & ((jj >= ii) if reverse else (jj <= ii))
    strict = chunk & ((jj > ii) if reverse else (jj < ii))
    tri = jnp.where(incl, 1.0, 0.0).astype(BF16)
    gc = sum(jnp.dot(tri, p, preferred_element_type=F32) for p in _split3(lg))
    last = 0 if reverse else c - 1
    glast = [gc[ci * c + last:ci * c + last + 1, :] for ci in range(nchunk)]
    gtot = jnp.concatenate([jnp.broadcast_to(g, (c, 256)) for g in glast], axis=0)
    eg = jnp.exp(gc)
    kb = k * beta
    wu_rhs = jnp.concatenate([kb * eg, v * beta], axis=1).astype(BF16)
    ktail = k * jnp.exp(gtot - gc)
    qdec = q * eg
    hi = gc.astype(BF16).astype(F32)
    r1 = gc - hi
    mid = r1.astype(BF16).astype(F32)
    lo = (r1 - mid).astype(BF16).astype(F32)
    lane = _lane_iota(256)
    l6 = lane & 63
    ex = jnp.where(l6 == 0, hi, jnp.where(l6 == 1, mid, jnp.where(l6 == 2, lo, jnp.where(l6 < 6, 1.0, 0.0))))
    ey = jnp.where(l6 < 3, 1.0, jnp.where(l6 == 3, -hi, jnp.where(l6 == 4, -mid, jnp.where(l6 == 5, -lo, 0.0))))
    ey = ey.astype(BF16)
    kbf = k.astype(BF16)
    heads = range(DN_HEADS)
    hm = [(lane >> 6) == h for h in heads]
    nt_dims = (((1,), (1,)), ((), ()))
    e = [lax.dot_general(jnp.where(hm[h], ex, 0.0).astype(BF16), ey, nt_dims, preferred_element_type=F32)
         for h in heads]
    decay = [jnp.exp(jnp.where(incl, x, NEG_BIG)) for x in e]
    kk = [_dot_nt(jnp.where(hm[h], kb, 0.0), kbf) for h in heads]
    tinv = _block_triangular_inverse([jnp.where(strict, kk[h] * decay[h], 0.0) for h in heads], ii, jj)
    wu = [jnp.dot(tinv[h].astype(BF16), wu_rhs, preferred_element_type=F32) for h in heads]
    w_all = sum(jnp.where(hm[h], wu[h][:, 0:256], 0.0) for h in heads)
    u_all = sum(jnp.where(hm[h], wu[h][:, 256:512], 0.0) for h in heads)
    qk = []
    if emit:
        qk = [(_dot_nt(jnp.where(hm[h], q, 0.0), kbf) * decay[h]).astype(BF16) for h in heads]
    rr = lax.broadcasted_iota(jnp.int32, (256, 256), 0) >> 6
    cc = lax.broadcasted_iota(jnp.int32, (256, 256), 1) >> 6
    bd = rr == cc
    vnew = [None] * nchunk
    ocross = [None] * nchunk
    for step in range(nchunk):
        ci = (nchunk - 1 - step) if reverse else step
        rows = slice(ci * c, (ci + 1) * c)
        s_old = st[...]
        sb = s_old.astype(BF16)
        vnew[ci] = u_all[rows] - jnp.dot(w_all[rows].astype(BF16), sb, preferred_element_type=F32)
        if emit:
            ocross[ci] = jnp.dot(qdec[rows].astype(BF16), sb, preferred_element_type=F32)
        st[...] = s_old * jnp.exp(glast[ci]) + jnp.where(bd, _dot_tn(ktail[rows], vnew[ci]), 0.0)
    if not emit:
        return None
    vn = jnp.concatenate(vnew, axis=0).astype(BF16)
    o = jnp.concatenate(ocross, axis=0)
    intra = [jnp.dot(qk[h], vn, preferred_element_type=F32) for h in heads]
    return o + sum(jnp.where(hm[h], intra[h], 0.0) for h in heads)


def _dn_kernel(*refs, reverse, nt, with_ctx_out, final):
    it = iter(refs)
    xc_ref, x_ref, xp_ref, xn_ref, cw_ref, alog_ref, dtb_ref = (next(it) for _ in range(7))
    if final:
        g_ref, of_ref = next(it), next(it)
        ofc_ref = next(it) if with_ctx_out else None
    o_ref = next(it)
    oc_ref = next(it) if with_ctx_out else None
    xext, st = next(it), next(it)
    s = pl.program_id(1)
    direction = 1 if reverse else 0
    alog, dtb = alog_ref[...], dtb_ref[...]

    def run(x, prev8, next8, emit):
        q, k, v, lg, beta = _dn_prepare(x, prev8, next8, cw_ref, alog, dtb, xext, direction=direction)
        return _dn_tile(q, k, v, lg, beta, st, reverse=reverse, emit=emit)

    def finish(o, x, other_ref, out_ref):
        if not final:
            out_ref[0] = o
            return
        o = o + other_ref[0]
        ms = _group_sum(o * o, _group_ones(256, 6)) * (1.0 / DN_HEAD_DIM)
        out_ref[0] = o * lax.rsqrt(ms + NORM_EPS) * g_ref[...] * _silu(x[:, 768:1024])

    zeros8 = jnp.zeros((8, 768), F32)

    @pl.when(s == 0)
    def _():
        st[...] = jnp.zeros_like(st)
        xc = xc_ref[0]
        o = run(xc, zeros8, zeros8, with_ctx_out)
        if with_ctx_out:
            finish(o, xc, ofc_ref if final else None, oc_ref)

    @pl.when(s > 0)
    def _():
        tile = (nt - s) if reverse else (s - 1)
        x = x_ref[0]
        prev8 = jnp.where(tile > 0, xp_ref[0], 0.0)
        next8 = jnp.where(tile < nt - 1, xn_ref[0], 0.0)
        o = run(x, prev8, next8, True)
        finish(o, x, of_ref if final else None, o_ref)


def _deltanet_direction(zb, zbc, conv_w, alog_x, dtb_x, norm_g, other, other_c, *, reverse, with_ctx_out):
    b, s, _ = zb.shape
    l = zbc.shape[1]
    tt = DN_TILE
    assert l == tt and s % tt == 0
    nt = s // tt
    final = other is not None
    r8 = tt // 8
    tile_of = (lambda st_: nt - 1 - jnp.maximum(st_ - 1, 0)) if reverse else (lambda st_: jnp.maximum(st_ - 1, 0))
    tok = lambda w_: pl.BlockSpec((1, tt, w_), lambda i, j: (i, tile_of(j), 0))
    ctx = lambda w_: pl.BlockSpec((1, l, w_), lambda i, j: (i, 0, 0))
    halo_p = pl.BlockSpec((1, 8, 768), lambda i, j: (i, jnp.maximum(tile_of(j) * r8 - 1, 0), 0))
    halo_n = pl.BlockSpec((1, 8, 768), lambda i, j: (i, jnp.minimum((tile_of(j) + 1) * r8, s // 8 - 1), 0))
    in_specs = [ctx(ZB_W), tok(ZB_W), halo_p, halo_n, _const_spec((8, 768)), _const_spec((1, 256)), _const_spec((1, 256))]
    args = [zbc, zb, zb, zb, conv_w, alog_x, dtb_x]
    if final:
        in_specs += [_const_spec((1, 256)), tok(256)]
        args += [norm_g, other]
        if with_ctx_out:
            in_specs.append(ctx(256))
            args.append(other_c)
    out_specs = [tok(256)]
    out_shape = [jax.ShapeDtypeStruct((b, s, 256), F32)]
    if with_ctx_out:
        out_specs.append(ctx(256))
        out_shape.append(jax.ShapeDtypeStruct((b, l, 256), F32))
    scratch = [pltpu.VMEM((tt + 16, 768), F32), pltpu.VMEM((256, 256), F32)]
    outs = pl.pallas_call(
        functools.partial(_dn_kernel, reverse=reverse, nt=nt, with_ctx_out=with_ctx_out, final=final),
        name="deltanet_bwd" if reverse else "deltanet_fwd",
        grid=(b, nt + 1),
        in_specs=in_specs,
        out_specs=out_specs,
        out_shape=out_shape,
        scratch_shapes=scratch,
        compiler_params=_params("parallel", "arbitrary"),
    )(*args)
    return (outs[0], outs[1]) if with_ctx_out else (outs[0], None)


def _deltanet(zb, zbc, conv_w, a_log, dt_bias, norm_g, with_ctx_out):
    cw = jnp.pad(conv_w, ((0, 8 - DN_CONV), (0, 0)))
    expand = lambda p: jnp.repeat(p, DN_HEAD_DIM, axis=-1)
    alog_x, dtb_x = expand(a_log), expand(dt_bias)
    g = jnp.tile(norm_g, DN_HEADS).reshape(1, 256)
    o_f, oc_f = _deltanet_direction(zb, zbc, cw, alog_x[0:1], dtb_x[0:1], None, None, None,
                                    reverse=False, with_ctx_out=with_ctx_out)
    return _deltanet_direction(zb, zbc, cw, alog_x[1:2], dtb_x[1:2], g, o_f, oc_f,
                               reverse=True, with_ctx_out=with_ctx_out)


def _ret_masks():
    lane_q = _lane_iota(128)
    lane_v = _lane_iota(256)
    rq = (lax.broadcasted_iota(jnp.int32, (128, 256), 0) >> 4) & 3
    cv = lax.broadcasted_iota(jnp.int32, (128, 256), 1) >> 6
    return (lane_q >> 4) & 3, lane_v >> 6, rq == cv


def _ret_ctx_state(kc, vc, lgq, bd, *, reverse):
    l = kc.shape[0]
    pos = lax.broadcasted_iota(jnp.int32, (l, 128), 0).astype(F32)
    wgt = pos if reverse else (l - 1.0 - pos)
    return jnp.where(bd, _dot_tn(kc * jnp.exp(wgt * lgq), vc), 0.0)


def _ret_state_update(r_old, k, v, lgq, lgv, bd, *, reverse):
    c = k.shape[0]
    pos = lax.broadcasted_iota(jnp.int32, (c, 128), 0).astype(F32)
    wgt = pos if reverse else (c - 1.0 - pos)
    return r_old * jnp.exp(c * lgv) + jnp.where(bd, _dot_tn(k * jnp.exp(wgt * lgq), v), 0.0)


def _ret_state_kernel(*refs, has_ctx):
    if has_ctx:
        k_ref, v_ref, kc_ref, vc_ref, l1q_ref, l1v_ref, o_ref, r_s = refs
    else:
        k_ref, v_ref, l1q_ref, l1v_ref, o_ref, r_s = refs
    _, _, bd = _ret_masks()
    lgq = jnp.log1p(-jnp.exp(l1q_ref[1:2, :]))
    lgv = jnp.log1p(-jnp.exp(l1v_ref[1:2, :]))

    @pl.when(pl.program_id(1) == 0)
    def _():
        if has_ctx:
            r_s[...] = _ret_ctx_state(kc_ref[0], vc_ref[0], lgq, bd, reverse=True)
        else:
            r_s[...] = jnp.zeros_like(r_s)

    r_old = r_s[...]
    o_ref[0, 0] = r_old
    r_s[...] = _ret_state_update(r_old, k_ref[0], v_ref[0], lgq, lgv, bd, reverse=True)


def _ret_main_kernel(*refs, has_ctx):
    if has_ctx:
        q_ref, k_ref, v_ref, g_ref, rb_ref, kc_ref, vc_ref, l1q_ref, l1v_ref, l1r_ref, ng_ref, o_ref, r_s = refs
    else:
        q_ref, k_ref, v_ref, g_ref, rb_ref, l1q_ref, l1v_ref, l1r_ref, ng_ref, o_ref, r_s = refs
    hq, hv, bd = _ret_masks()
    lgq = jnp.log1p(-jnp.exp(l1q_ref[...]))
    lgv = jnp.log1p(-jnp.exp(l1v_ref[...]))
    lgr = jnp.log1p(-jnp.exp(l1r_ref[...]))

    @pl.when(pl.program_id(1) == 0)
    def _():
        if has_ctx:
            r_s[...] = _ret_ctx_state(kc_ref[0], vc_ref[0], lgq[0:1], bd, reverse=False)
        else:
            r_s[...] = jnp.zeros_like(r_s)

    q = q_ref[0] * (RET_QK_DIM ** -0.5)
    k = k_ref[0]
    v = v_ref[0]
    c = q.shape[0]
    ii = lax.broadcasted_iota(jnp.int32, (c, c), 0)
    jj = lax.broadcasted_iota(jnp.int32, (c, c), 1)
    rel = (ii - jj).astype(F32)
    o = None
    for h in range(RET_HEADS):
        sc = _dot_nt(jnp.where(hq == h, q, 0.0), k)
        d_f = jnp.where(ii >= jj, jnp.exp(jnp.maximum(rel, 0.0) * lgr[h:h + 1, :]), 0.0)
        d_b = jnp.where(jj >= ii, jnp.exp(jnp.maximum(-rel, 0.0) * lgr[4 + h:5 + h, :]), 0.0)
        t = jnp.where(hv == h, _dot(sc * (d_f + d_b), v), 0.0)
        o = t if o is None else o + t
    pos = lax.broadcasted_iota(jnp.int32, (c, 128), 0).astype(F32)
    r_old = r_s[...]
    o = o + _dot(q * jnp.exp((pos + 1.0) * lgq[0:1]), r_old) + _dot(q * jnp.exp((c - pos) * lgq[1:2]), rb_ref[0, 0])
    r_s[...] = _ret_state_update(r_old, k, v, lgq[0:1], lgv[0:1], bd, reverse=False)
    ones64 = _group_ones(256, 6)
    mu = _group_sum(o, ones64) * (1.0 / RET_V_DIM)
    d = o - mu
    var = _group_sum(d * d, ones64) * (1.0 / RET_V_DIM)
    o_ref[0] = d * lax.rsqrt(var + NORM_EPS) * ng_ref[...] * _silu(g_ref[0])


def _retention(zc, zcc, log1m_gamma, norm_g):
    b, t, _ = zc.shape
    c = min(RET_TILE, t)
    assert t % c == 0
    nc = t // c
    has_ctx = zcc is not None
    l1q = jnp.tile(jnp.repeat(log1m_gamma, 16, axis=-1), (1, 2))
    l1v = jnp.repeat(log1m_gamma, 64, axis=-1)
    l1r = jnp.broadcast_to(log1m_gamma.reshape(8, 1), (8, c))
    col = lambda cb, w_, f: pl.BlockSpec((1, c, w_), lambda i, n: (i, f(n), cb))
    fwd = lambda n: n
    bwd = lambda n: nc - 1 - n
    ctx_specs, ctx_args = [], []
    if has_ctx:
        l = zcc.shape[1]
        ctx_specs = [pl.BlockSpec((1, l, 128), lambda i, n: (i, 0, 1)), pl.BlockSpec((1, l, 256), lambda i, n: (i, 0, 1))]
        ctx_args = [zcc, zcc]
    rb = pl.pallas_call(
        functools.partial(_ret_state_kernel, has_ctx=has_ctx),
        name="retention_state",
        grid=(b, nc),
        in_specs=[col(1, 128, bwd), col(1, 256, bwd)] + ctx_specs + [_const_spec((2, 128)), _const_spec((2, 256))],
        out_specs=pl.BlockSpec((1, 1, 128, 256), lambda i, n: (i, bwd(n), 0, 0)),
        out_shape=jax.ShapeDtypeStruct((b, nc, 128, 256), F32),
        scratch_shapes=[pltpu.VMEM((128, 256), F32)],
        compiler_params=_params("parallel", "arbitrary"),
    )(zc, zc, *ctx_args, l1q, l1v)
    return pl.pallas_call(
        functools.partial(_ret_main_kernel, has_ctx=has_ctx),
        name="retention_main",
        grid=(b, nc),
        in_specs=[col(0, 128, fwd), col(1, 128, fwd), col(1, 256, fwd), col(2, 256, fwd),
                  pl.BlockSpec((1, 1, 128, 256), lambda i, n: (i, n, 0, 0))] + ctx_specs
                 + [_const_spec((2, 128)), _const_spec((2, 256)), _const_spec((8, c)), _const_spec((1, 256))],
        out_specs=pl.BlockSpec((1, c, 256), lambda i, n: (i, n, 0)),
        out_shape=jax.ShapeDtypeStruct((b, t, 256), F32),
        scratch_shapes=[pltpu.VMEM((128, 256), F32)],
        compiler_params=_params("parallel", "arbitrary"),
    )(zc, zc, zc, zc, rb, *ctx_args, l1q, l1v, l1r, norm_g.reshape(1, 256))


def _mla_weights(w_uq, w_ukv):
    dq = MLA_NOPE_DIM + MLA_ROPE_DIM
    qa = np.full((MLA_HEADS, 128), -1, np.int32)
    qb = np.full((MLA_HEADS, 128), -1, np.int32)
    kk = np.full((MLA_HEADS, 128), -1, np.int32)
    vv = np.zeros((MLA_HEADS, 64), np.int32)
    for h in range(MLA_HEADS):
        qa[h, 0:96] = h * dq + np.arange(96)
        qb[h, 64:80] = h * dq + 64 + 16 + np.arange(16)
        qb[h, 80:96] = h * dq + 64 + np.arange(16)
        kk[h, 0:64] = h * 128 + np.arange(64)
        vv[h] = h * 128 + 64 + np.arange(64)

    def take(w, idx):
        idx = idx.reshape(-1)
        out = jnp.take(w, jnp.asarray(np.maximum(idx, 0)), axis=1) * jnp.asarray((idx >= 0).astype(np.float32))[None, :]
        return out.astype(BF16)

    return take(w_uq, qa), take(w_uq, qb), take(w_ukv, kk), take(w_ukv, vv)


def _mla_prep_kernel(*refs, rope):
    if rope:
        zd_ref, qn_ref, kn_ref, wqa_ref, wqb_ref, wk_ref, wv_ref, cd_ref, sd_ref, q_ref, k_ref, v_ref = refs
    else:
        zd_ref, qn_ref, kn_ref, wqa_ref, wk_ref, wv_ref, q_ref, k_ref, v_ref = refs
    zd = zd_ref[0]
    cq, ckv, kr = zd[:, 0:256], zd[:, 256:384], zd[:, 384:512]
    nq = cq * lax.rsqrt(jnp.mean(cq * cq, axis=-1, keepdims=True) + NORM_EPS) * qn_ref[...]
    nkv = ckv * lax.rsqrt(jnp.mean(ckv * ckv, axis=-1, keepdims=True) + NORM_EPS) * kn_ref[...]
    scale = (MLA_NOPE_DIM + MLA_ROPE_DIM) ** -0.5 * LOG2_E
    qa = _dot(nq, wqa_ref[...])
    if rope:
        qb = _dot(nq, wqb_ref[...])
        cd, sd = cd_ref[...], sd_ref[...]
    kall = _dot(nkv, wk_ref[...])
    for h in range(MLA_HEADS):
        sl = slice(h * 128, (h + 1) * 128)
        qh = qa[:, sl]
        if rope:
            qh = qh * cd + qb[:, sl] * sd
        q_ref[0, :, sl] = (qh * scale).astype(BF16)
        k_ref[0, :, sl] = (kall[:, sl] + kr).astype(BF16)
    v_ref[0] = _dot(nkv, wv_ref[...]).astype(BF16)


def _mla_prep(zd, q_norm, kv_norm, weights, tables):
    b, t, _ = zd.shape
    tm = min(TOK_TILE, t)
    rope = tables is not None
    wqa, wqb, wk, wv = weights
    tok = lambda w_: pl.BlockSpec((1, tm, w_), lambda i, j: (i, j, 0))
    tab = pl.BlockSpec((tm, 128), lambda i, j: (j, 0))
    in_specs = [tok(ZD_W), _const_spec((1, 256)), _const_spec((1, 128)), _const_spec((256, 512))]
    args = [zd, q_norm.reshape(1, 256), kv_norm.reshape(1, 128), wqa]
    if rope:
        in_specs.append(_const_spec((256, 512)))
        args.append(wqb)
    in_specs += [_const_spec((128, 512)), _const_spec((128, 256))]
    args += [wk, wv]
    if rope:
        in_specs += [tab, tab]
        args += list(tables)
    return pl.pallas_call(
        functools.partial(_mla_prep_kernel, rope=rope),
        name="mla_prep",
        grid=(b, t // tm),
        in_specs=in_specs,
        out_specs=[tok(512), tok(512), tok(256)],
        out_shape=[jax.ShapeDtypeStruct((b, t, 512), BF16), jax.ShapeDtypeStruct((b, t, 512), BF16),
                   jax.ShapeDtypeStruct((b, t, 256), BF16)],
        compiler_params=_params("parallel", "parallel"),
    )(*args)


def _mla_attn_kernel(*refs, two_segments):
    if two_segments:
        q_ref, k1_ref, v1_ref, k2_ref, v2_ref, o_ref, s_scr = refs
        segments = [(k1_ref, v1_ref), (k2_ref, v2_ref)]
    else:
        q_ref, k2_ref, v2_ref, o_ref, s_scr = refs
        segments = [(k2_ref, v2_ref)]
    chunks = []
    off = 0
    for k_ref, v_ref in segments:
        n = k_ref.shape[1]
        step = min(MLA_KEY_CHUNK, n)
        for r0 in range(0, n, step):
            chunks.append((k_ref, v_ref, r0, step, off))
            off += step
    tq = q_ref.shape[1]
    lane = _lane_iota(256)
    nt_dims = (((1,), (1,)), ((), ()))
    total = None
    for h in range(MLA_HEADS):
        sl = slice(h * 128, (h + 1) * 128)
        qh = q_ref[0, :, sl]
        mlane = None
        for k_ref, _, r0, rows, col in chunks:
            s = lax.dot_general(qh, k_ref[0, r0:r0 + rows, sl], nt_dims, preferred_element_type=F32)
            s_scr[:, col:col + rows] = s
            for j in range(rows // 128):
                tile = s[:, j * 128:(j + 1) * 128]
                mlane = tile if mlane is None else jnp.maximum(mlane, tile)
        m = jnp.max(mlane, axis=-1, keepdims=True)
        llane = jnp.zeros((tq, 128), F32)
        acc = None
        for _, v_ref, r0, rows, col in chunks:
            p = jnp.exp2(s_scr[:, col:col + rows] - m)
            for j in range(rows // 128):
                llane = llane + p[:, j * 128:(j + 1) * 128]
            o = jnp.dot(p.astype(BF16), v_ref[0, r0:r0 + rows, :], preferred_element_type=F32)
            acc = o if acc is None else acc + o
        den = jnp.sum(llane, axis=-1, keepdims=True)
        o = acc * jnp.where((lane >> 6) == h, 1.0 / den, 0.0)
        total = o if total is None else total + o
    o_ref[0] = total


def _mla_attend(q, kv_latent, kv_ctx):
    b, t, _ = q.shape
    tq = min(MLA_TQ, t)
    two = kv_latent is not None
    full = lambda n, w_: pl.BlockSpec((1, n, w_), lambda i, j: (i, 0, 0))
    in_specs = [pl.BlockSpec((1, tq, 512), lambda i, j: (i, j, 0))]
    args = [q]
    n_keys = 0
    for kv in ((kv_latent, kv_ctx) if two else (kv_ctx,)):
        n = kv[0].shape[1]
        assert n % min(MLA_KEY_CHUNK, n) == 0 and min(MLA_KEY_CHUNK, n) % 128 == 0
        n_keys += n
        in_specs += [full(n, 512), full(n, 256)]
        args += list(kv)
    return pl.pallas_call(
        functools.partial(_mla_attn_kernel, two_segments=two),
        name="mla_attention",
        grid=(b, t // tq),
        in_specs=in_specs,
        out_specs=pl.BlockSpec((1, tq, 256), lambda i, j: (i, j, 0)),
        out_shape=jax.ShapeDtypeStruct((b, t, 256), F32),
        scratch_shapes=[pltpu.VMEM((tq, n_keys), F32)],
        compiler_params=_params("parallel", "arbitrary"),
    )(*args)


def _out_proj_kernel(ya_ref, yb_ref, yc_ref, yd_ref, x_ref, g_ref, w_ref, lg_ref, lb_ref, o_ref):
    y = jnp.concatenate([ya_ref[0], yb_ref[0], yc_ref[0], yd_ref[0]], axis=-1)
    proj = _dot(y, w_ref[...])
    o_ref[0] = _layer_norm(DEEPNORM_ALPHA * x_ref[0] + g_ref[0] * proj, lg_ref[...], lb_ref[...])


def _out_proj(ys, x, gate, w, ln_g, ln_b):
    b, t, d = x.shape
    tm = min(TOK_TILE, t)
    tok = lambda w_: pl.BlockSpec((1, tm, w_), lambda i, j: (i, j, 0))
    vec = pl.BlockSpec((1, 1, d), lambda i, j: (i, 0, 0))
    return pl.pallas_call(
        _out_proj_kernel,
        name="out_proj_ln",
        grid=(b, t // tm),
        in_specs=[tok(256)] * 4 + [tok(d), vec, _const_spec((d, d)), _const_spec((1, d)), _const_spec((1, d))],
        out_specs=tok(d),
        out_shape=jax.ShapeDtypeStruct((b, t, d), F32),
        compiler_params=_params("parallel", "parallel"),
    )(*ys, x, gate, w, ln_g.reshape(1, d), ln_b.reshape(1, d))


def _mlp_kernel(x_ref, sc_ref, sh_ref, g_ref, w1_ref, w2_ref, lg_ref, lb_ref, o_ref):
    x = x_ref[0]
    h = (x * (1.0 + sc_ref[0]) + sh_ref[0]).astype(BF16)
    acc = None
    step = 1024
    for j in range(D_FF // step):
        u = jnp.maximum(jnp.dot(h, w1_ref[:, j * step:(j + 1) * step], preferred_element_type=F32), 0.0)
        t = jnp.dot((u * u).astype(BF16), w2_ref[j * step:(j + 1) * step, :], preferred_element_type=F32)
        acc = t if acc is None else acc + t
    o_ref[0] = _layer_norm(DEEPNORM_ALPHA * x + g_ref[0] * acc, lg_ref[...], lb_ref[...])


def _mlp(x, sc, sh, gate, w1, w2, ln_g, ln_b):
    b, t, d = x.shape
    tm = min(TOK_TILE, t)
    tok = pl.BlockSpec((1, tm, d), lambda i, j: (i, j, 0))
    vec = pl.BlockSpec((1, 1, d), lambda i, j: (i, 0, 0))
    return pl.pallas_call(
        _mlp_kernel,
        name="mlp_ln",
        grid=(b, t // tm),
        in_specs=[tok, vec, vec, vec, _const_spec((d, D_FF)), _const_spec((D_FF, d)), _const_spec((1, d)),
                  _const_spec((1, d))],
        out_specs=tok,
        out_shape=jax.ShapeDtypeStruct((b, t, d), F32),
        compiler_params=_params("parallel", "parallel"),
    )(x, sc, sh, gate, w1, w2, ln_g.reshape(1, d), ln_b.reshape(1, d))


def _rope_tables(n_tok):
    rows = n_tok // GRID_W
    freqs = lambda dim: ROPE_BASE ** (-jnp.arange(0, dim, 2, dtype=F32) / dim)
    row = jnp.broadcast_to(jnp.arange(rows, dtype=F32)[:, None], (rows, GRID_W)).reshape(-1)
    col = jnp.broadcast_to(jnp.arange(GRID_W, dtype=F32)[None, :], (rows, GRID_W)).reshape(-1)

    def axial(rot_dim):
        inv = freqs(rot_dim // 2)
        return jnp.concatenate([row[:, None] * inv, col[:, None] * inv], axis=-1)

    ang_a = axial(SWA_HEAD_DIM)
    ang_c = jnp.arange(n_tok, dtype=F32)[:, None] * freqs(RET_QK_DIM)
    ang_d = axial(MLA_ROPE_DIM)
    pair = lambda a, reps: (jnp.tile(jnp.cos(a), (1, 2 * reps)),
                            jnp.concatenate([-jnp.tile(jnp.sin(a), (1, reps)), jnp.tile(jnp.sin(a), (1, reps))], axis=-1))
    ca, sa = pair(ang_a, 2)
    cc, sc = pair(ang_c, 4)
    one = jnp.ones((n_tok, 64), F32)
    zero = jnp.zeros((n_tok, 64), F32)
    cd = jnp.concatenate([one, jnp.cos(ang_d), jnp.cos(ang_d), one[:, :32]], axis=-1)
    sd = jnp.concatenate([zero, -jnp.sin(ang_d), jnp.sin(ang_d), zero[:, :32]], axis=-1)
    return ca, sa, cc, sc, cd, sd


def kernel(x, c, ctx, c_ctx, ada_w, ada_b, w_in, swa_sink, dn_conv_w, dn_a_log, dn_dt_bias, dn_norm_g,
           ret_log1m_gamma, ret_norm_g, mla_q_norm, mla_w_uq, mla_kv_norm, mla_w_ukv, w_out, ln1_g, ln1_b,
           w_ff1, w_ff2, ln2_g, ln2_b):
    b, n_tok, d = x.shape
    tables = _rope_tables(n_tok)
    cc = jnp.concatenate([c, c_ctx[None, :], jnp.zeros((7, d), F32)], axis=0)
    perm_a = np.concatenate([np.arange(64) + 64 * hd for hd in (0, 2, 1, 3)])
    out_rows = np.concatenate([perm_a, np.arange(256, 1024)])
    xc = ctx
    for layer in range(DEPTH):
        with_ctx_out = layer < DEPTH - 1
        mod = _ada_modulation(cc, ada_w[layer], ada_b[layer])
        sh1, sc1, g1, sh2, sc2, g2 = jnp.split(mod[:b, None, :], 6, axis=-1)
        cmod = jnp.broadcast_to(mod[b][None, None, :], (b, 1, 6 * d))
        csh1, csc1, cg1, csh2, csc2, cg2 = jnp.split(cmod, 6, axis=-1)
        w_in_p = _permute_in_proj(w_in[layer])
        za, zb, zc, zd = _in_proj(x, sc1, sh1, w_in_p, tables)
        zac, zbc, zcc, zdc = _in_proj(xc, csc1, csh1, w_in_p, None)
        sink = swa_sink[layer]
        ya = _swa_local(za, zac, sink)
        yb, ybc = _deltanet(zb, zbc, dn_conv_w[layer], dn_a_log[layer], dn_dt_bias[layer], dn_norm_g[layer],
                            with_ctx_out)
        yr = _retention(zc, zcc, ret_log1m_gamma[layer], ret_norm_g[layer])
        mla_w = _mla_weights(mla_w_uq[layer], mla_w_ukv[layer])
        qd, kd, vd = _mla_prep(zd, mla_q_norm[layer], mla_kv_norm[layer], mla_w, tables[4:6])
        qdc, kdc, vdc = _mla_prep(zdc, mla_q_norm[layer], mla_kv_norm[layer], mla_w, None)
        yd = _mla_attend(qd, (kd, vd), (kdc, vdc))
        w_out_p = jnp.take(w_out[layer], jnp.asarray(out_rows), axis=0).astype(BF16)
        w1 = w_ff1[layer].astype(BF16)
        w2 = w_ff2[layer].astype(BF16)
        x_new = _out_proj((ya, yb, yr, yd), x, g1, w_out_p, ln1_g[layer], ln1_b[layer])
        x_new = _mlp(x_new, sc2, sh2, g2, w1, w2, ln2_g[layer], ln2_b[layer])
        if with_ctx_out:
            yac = _swa_ctx(zac, sink)
            yrc = _retention(zcc, None, ret_log1m_gamma[layer], ret_norm_g[layer])
            ydc = _mla_attend(qdc, None, (kdc, vdc))
            xc_new = _out_proj((yac, ybc, yrc, ydc), xc, cg1, w_out_p, ln1_g[layer], ln1_b[layer])
            xc = _mlp(xc_new, csc2, csh2, cg2, w1, w2, ln2_g[layer], ln2_b[layer])
        x = x_new
    return x
```

```python
import functools
import math

import numpy as np
import jax
import jax.numpy as jnp
from jax import lax
from jax.experimental import pallas as pl
from jax.experimental.pallas import tpu as pltpu

F32 = jnp.float32
BF16 = jnp.bfloat16

D_MODEL = 1024
DEPTH = 2
GRID_W = 64
SWA_HEADS, SWA_KV_HEADS, SWA_HEAD_DIM, SWA_WINDOW = 4, 2, 64, 128
DN_HEADS, DN_HEAD_DIM, DN_CONV, DN_CHUNK = 4, 64, 5, 64
RET_HEADS, RET_QK_DIM, RET_V_DIM = 4, 32, 64
MLA_HEADS, MLA_Q_RANK, MLA_KV_RANK, MLA_NOPE_DIM, MLA_ROPE_DIM, MLA_V_DIM = 4, 256, 128, 64, 32, 64
D_FF = 4 * D_MODEL
ROPE_BASE = 10000.0
NORM_EPS = 1e-6
LN_EPS = 1e-5
DEEPNORM_ALPHA = (2 * DEPTH) ** 0.25
IN_SPLITS = (256, 128, 128, 768, 256, 16, 128, 128, 256, 256, 256, 128, 32)

ZA_W, ZB_W, ZC_W, ZD_W, KRS_W = 512, 1152, 768, 512, 128
Z_OFF = np.cumsum((0, ZA_W, ZB_W, ZC_W, ZD_W, KRS_W))
Z_W = int(Z_OFF[-1])

VMEM_LIMIT = 48 * 1024 * 1024
NEG_BIG = -1e30

SWA_QTILE = 256
DN_TILE = 256
DN_ROWS = 4
RET_TILE = 256
MLA_TQ = 256
MLA_KEY_CHUNK = 512
MLA_ROW_BLOCK = 32
LOG2_E = 1.4426950408889634
TOK_TILE = 512
FF_CHUNK = 1024


def _dot(a, b):
    return jnp.dot(a.astype(BF16), b.astype(BF16), preferred_element_type=F32)


def _dot_nt(a, b):
    return lax.dot_general(a.astype(BF16), b.astype(BF16), (((1,), (1,)), ((), ())), preferred_element_type=F32)


def _dot_tn(a, b):
    return lax.dot_general(a.astype(BF16), b.astype(BF16), (((0,), (0,)), ((), ())), preferred_element_type=F32)


def _split2(a):
    hi = a.astype(BF16)
    lo = (a - hi.astype(F32)).astype(BF16)
    return hi, lo


def _split3(a):
    hi = a.astype(BF16)
    r = a - hi.astype(F32)
    mid = r.astype(BF16)
    lo = (r - mid.astype(F32)).astype(BF16)
    return hi, mid, lo


def _dot_sel(a, sel):
    return sum(jnp.dot(p, sel, preferred_element_type=F32) for p in _split3(a))


def _dot_sel_nt(sel, a):
    dn = (((1,), (1,)), ((), ()))
    return sum(lax.dot_general(sel, p, dn, preferred_element_type=F32) for p in _split3(a))


def _dot_hp(a, b):
    ah, al = _split2(a)
    bh, bl = _split2(b)
    f = lambda x, y: jnp.dot(x, y, preferred_element_type=F32)
    return f(ah, bh) + (f(ah, bl) + f(al, bh))


def _group_ones(width, shift):
    r = lax.broadcasted_iota(jnp.int32, (width, width), 0) >> shift
    c = lax.broadcasted_iota(jnp.int32, (width, width), 1) >> shift
    return jnp.where(r == c, 1.0, 0.0).astype(BF16)


def _group_sum(x, ones_bd):
    hi, lo = _split2(x)
    return jnp.dot(hi, ones_bd, preferred_element_type=F32) + jnp.dot(lo, ones_bd, preferred_element_type=F32)


def _silu(x):
    return x * jax.nn.sigmoid(x)


def _layer_norm(v, g, b):
    mu = jnp.mean(v, axis=-1, keepdims=True)
    d = v - mu
    var = jnp.mean(d * d, axis=-1, keepdims=True)
    return d * lax.rsqrt(var + LN_EPS) * g + b


def _lane_iota(width):
    return lax.broadcasted_iota(jnp.int32, (1, width), 1)


def _params(*sem):
    return pltpu.CompilerParams(dimension_semantics=sem, vmem_limit_bytes=VMEM_LIMIT)


def _const_spec(shape):
    nd = len(shape)
    return pl.BlockSpec(shape, lambda *_: (0,) * nd)


def _resident_spec(shape):
    nd = len(shape)
    return pl.BlockSpec(shape, lambda *_: (0,) * nd, pipeline_mode=pl.Buffered(1))


def _ada_kernel(c_ref, w_ref, b_ref, o_ref):
    o_ref[...] = _dot(_silu(c_ref[...]), w_ref[...]) + b_ref[...]


def _ada_modulation(cc, w, b):
    rows, d = cc.shape
    n = w.shape[1]
    tn = 1536
    return pl.pallas_call(
        _ada_kernel,
        name="ada_modulation",
        grid=(n // tn,),
        in_specs=[pl.BlockSpec((rows, d), lambda j: (0, 0)), pl.BlockSpec((d, tn), lambda j: (0, j)),
                  pl.BlockSpec((1, tn), lambda j: (0, j))],
        out_specs=pl.BlockSpec((rows, tn), lambda j: (0, j)),
        out_shape=jax.ShapeDtypeStruct((rows, n), F32),
        compiler_params=_params("parallel"),
    )(cc, w, b.reshape(1, n))


def _in_proj_columns():
    off = np.cumsum((0,) + IN_SPLITS)
    aq, ak, av, bqkv, bz, bab, cq, ck, cv, cg, dcq, dckv, dkr = (int(o) for o in off[:-1])
    ar = lambda base, n: list(range(base, base + n))
    cols = []
    for g in range(2):
        for half in range(2):
            for h in range(2):
                cols += ar(aq + (2 * h + g) * 64 + half * 32, 32)
    for half in range(2):
        for h in range(2):
            cols += ar(ak + h * 64 + half * 32, 32)
    cols += ar(av, 128)
    cols += ar(bqkv, 768) + ar(bz, 256) + ar(bab, 16) + [-1] * 112
    for base in (cq, ck):
        for half in range(2):
            for h in range(4):
                cols += ar(base + h * 32 + half * 16, 16)
    cols += ar(cv, 256) + ar(cg, 256)
    cols += ar(dcq, 256) + ar(dckv, 128)
    cols += [-1] * 64 + ar(dkr, 32) + [-1] * 32
    cols += [-1] * 64 + ar(dkr + 16, 16) + ar(dkr, 16) + [-1] * 32
    cols = np.asarray(cols, np.int32)
    assert cols.shape[0] == Z_W
    return cols


_IN_COLS = _in_proj_columns()


def _permute_in_proj(w_in):
    idx = jnp.asarray(np.maximum(_IN_COLS, 0))
    keep = jnp.asarray((_IN_COLS >= 0).astype(np.float32))
    return (jnp.take(w_in, idx, axis=1) * keep[None, :]).astype(BF16)


def _in_proj_kernel(*refs, rope):
    if rope:
        x_ref, sc_ref, sh_ref, w_ref, ca_ref, sa_ref, cc_ref, sc2_ref, cd_ref, sd_ref, oa, ob, oc, od = refs
    else:
        x_ref, sc_ref, sh_ref, w_ref, oa, ob, oc, od = refs
    h = x_ref[0] * (1.0 + sc_ref[0]) + sh_ref[0]
    z = _dot(h, w_ref[...])
    za = z[:, Z_OFF[0]:Z_OFF[1]]
    zc = z[:, Z_OFF[2]:Z_OFF[3]]
    zd = z[:, Z_OFF[3]:Z_OFF[4]]
    ob[0] = z[:, Z_OFF[1]:Z_OFF[2]]
    if not rope:
        oa[0] = za
        oc[0] = zc
        od[0] = zd
        return
    krs = z[:, Z_OFF[4]:Z_OFF[5]]

    def rot(t, c, s):
        return t * c + pltpu.roll(t, 64, 1) * s

    ca, sa = ca_ref[...], sa_ref[...]
    for j in range(3):
        oa[0, :, j * 128:(j + 1) * 128] = rot(za[:, j * 128:(j + 1) * 128], ca, sa)
    oa[0, :, 384:512] = za[:, 384:512]
    cc, sc2 = cc_ref[...], sc2_ref[...]
    for j in range(2):
        oc[0, :, j * 128:(j + 1) * 128] = rot(zc[:, j * 128:(j + 1) * 128], cc, sc2)
    oc[0, :, 256:768] = zc[:, 256:768]
    od[0, :, 0:384] = zd[:, 0:384]
    od[0, :, 384:512] = zd[:, 384:512] * cd_ref[...] + krs * sd_ref[...]


def _in_proj(x, sc, sh, w, tables):
    b, t, d = x.shape
    tm = min(TOK_TILE, t)
    rope = tables is not None
    tok = lambda w_: pl.BlockSpec((1, tm, w_), lambda i, j: (i, j, 0))
    vec = pl.BlockSpec((1, 1, d), lambda i, j: (i, 0, 0))
    tab = pl.BlockSpec((tm, 128), lambda i, j: (j, 0))
    in_specs = [tok(d), vec, vec, _resident_spec((d, Z_W))] + ([tab] * 6 if rope else [])
    args = [x, sc, sh, w] + (list(tables) if rope else [])
    widths = (ZA_W, ZB_W, ZC_W, ZD_W)
    return pl.pallas_call(
        functools.partial(_in_proj_kernel, rope=rope),
        name="in_proj_rope" if rope else "in_proj_ctx",
        grid=(b, t // tm),
        in_specs=in_specs,
        out_specs=[tok(w_) for w_ in widths],
        out_shape=[jax.ShapeDtypeStruct((b, t, w_), F32) for w_ in widths],
        compiler_params=_params("parallel", "parallel"),
    )(*args)


def _swa_heads(sink_ref, qg, keys, vals, mask):
    lane = _lane_iota(128)
    scale = SWA_HEAD_DIM ** -0.5
    kb = keys.astype(BF16)
    vb = vals.astype(BF16)
    hg = [(h, g) for g in range(2) for h in range(2)]
    s = [_dot_nt(jnp.where(((lane >> 5) & 1) == h, qg[g] * scale, 0.0), kb) for h, g in hg]
    if mask is not None:
        s = [jnp.where(mask, x, NEG_BIG) for x in s]
    snk = [sink_ref[2 * h + g] for h, g in hg]
    m = [jnp.maximum(jnp.max(x, axis=-1, keepdims=True), k_) for x, k_ in zip(s, snk)]
    p = [jnp.exp(x - y) for x, y in zip(s, m)]
    den = [jnp.sum(x, axis=-1, keepdims=True) + jnp.exp(k_ - y) for x, k_, y in zip(p, snk, m)]
    o = [jnp.dot(x.astype(BF16), vb, preferred_element_type=F32) for x in p]
    o = [x * jnp.where((lane >> 6) == h, 1.0 / d, 0.0) for x, d, (h, g) in zip(o, den, hg)]
    return jnp.concatenate([o[0] + o[1], o[2] + o[3]], axis=-1)


def _swa_local_kernel(sink_ref, q_ref, kp_ref, k0_ref, kn_ref, vp_ref, v0_ref, vn_ref, kc_ref, vc_ref, o_ref, *, nb):
    n = pl.program_id(1)
    w = SWA_WINDOW
    keys = jnp.concatenate([kp_ref[0], k0_ref[0], kn_ref[0], kc_ref[0]], axis=0)
    vals = jnp.concatenate([vp_ref[0], v0_ref[0], vn_ref[0], vc_ref[0]], axis=0)
    nk = keys.shape[0]
    qt = q_ref.shape[1]
    i = lax.broadcasted_iota(jnp.int32, (qt, nk), 0)
    j = lax.broadcasted_iota(jnp.int32, (qt, nk), 1)
    lo = jnp.where(n > 0, 0, w)
    hi = jnp.where(n < nb - 1, qt + 2 * w, qt + w)
    band = (j >= i) & (j <= i + 2 * w) & (j >= lo) & (j < hi)
    mask = band | (j >= qt + 2 * w)
    q = q_ref[0]
    o_ref[0] = _swa_heads(sink_ref, (q[:, :128], q[:, 128:]), keys, vals, mask)


def _swa_ctx_kernel(sink_ref, q_ref, kc_ref, vc_ref, o_ref):
    q = q_ref[0]
    o_ref[0] = _swa_heads(sink_ref, (q[:, :128], q[:, 128:]), kc_ref[0], vc_ref[0], None)


_SMEM_SPEC = pl.BlockSpec(memory_space=pltpu.SMEM)


def _swa_local(za, zac, sink):
    b, s, _ = za.shape
    l = zac.shape[1]
    w = SWA_WINDOW
    qt = min(SWA_QTILE, s)
    r = qt // w
    nb = s // qt
    blk = lambda col, f: pl.BlockSpec((1, w, 128), lambda i, n: (i, f(n), col))
    prev = lambda n: jnp.maximum(n * r - 1, 0)
    nxt = lambda n: jnp.minimum((n + 1) * r, s // w - 1)
    own = lambda col: pl.BlockSpec((1, qt, 128), lambda i, n: (i, n, col))
    ctx = lambda col: pl.BlockSpec((1, l, 128), lambda i, n: (i, 0, col))
    return pl.pallas_call(
        functools.partial(_swa_local_kernel, nb=nb),
        name="swa_local",
        grid=(b, nb),
        in_specs=[_SMEM_SPEC, pl.BlockSpec((1, qt, 256), lambda i, n: (i, n, 0)),
                  blk(2, prev), own(2), blk(2, nxt), blk(3, prev), own(3), blk(3, nxt), ctx(2), ctx(3)],
        out_specs=pl.BlockSpec((1, qt, 256), lambda i, n: (i, n, 0)),
        out_shape=jax.ShapeDtypeStruct((b, s, 256), F32),
        compiler_params=_params("parallel", "parallel"),
    )(sink, za, za, za, za, za, za, za, zac, zac)


def _swa_ctx(zac, sink):
    b, l, _ = zac.shape
    ctx = lambda col, w_: pl.BlockSpec((1, l, w_), lambda i: (i, 0, col))
    return pl.pallas_call(
        _swa_ctx_kernel,
        name="swa_ctx",
        grid=(b,),
        in_specs=[_SMEM_SPEC, ctx(0, 256), ctx(2, 128), ctx(3, 128)],
        out_specs=pl.BlockSpec((1, l, 256), lambda i: (i, 0, 0)),
        out_shape=jax.ShapeDtypeStruct((b, l, 256), F32),
        compiler_params=_params("parallel"),
    )(sink, zac, zac, zac)


def _dn_masks(tt, reverse):
    ii = lax.broadcasted_iota(jnp.int32, (tt, tt), 0)
    jj = lax.broadcasted_iota(jnp.int32, (tt, tt), 1)
    chunk = (ii >> 6) == (jj >> 6)
    incl = chunk & ((jj >= ii) if reverse else (jj <= ii))
    strict = chunk & ((jj > ii) if reverse else (jj < ii))
    lane = _lane_iota(256)
    rr = lax.broadcasted_iota(jnp.int32, (256, 256), 0) >> 6
    cc = lax.broadcasted_iota(jnp.int32, (256, 256), 1) >> 6
    return dict(ii=ii, jj=jj, incl=incl, strict=strict, tri=jnp.where(incl, 1.0, 0.0).astype(BF16), lane=lane,
                hm=[(lane >> 6) == h for h in range(DN_HEADS)], bd=rr == cc)


def _dn_row_prep(env, x, prev8, next8, cw_ref, alog, dtb, xext, masks, *, direction, reverse):
    t = x.shape[0]
    c = DN_CHUNK
    nchunk = t // c
    xext[0:8, :] = prev8
    xext[8:8 + t, :] = x[:, 0:768]
    xext[8 + t:16 + t, :] = next8
    pad = DN_CONV // 2
    y = None
    for kk in range(DN_CONV):
        term = cw_ref[kk:kk + 1, :] * xext[pl.ds(8 - pad + kk, t), :]
        y = term if y is None else y + term
    yield
    y = _silu(y)
    yield
    ones64 = _group_ones(256, 6)
    q, k, v = y[:, 0:256], y[:, 256:512], y[:, 512:768]
    q = q * lax.rsqrt(_group_sum(q * q, ones64) + NORM_EPS) * (DN_HEAD_DIM ** -0.5)
    k = k * lax.rsqrt(_group_sum(k * k, ones64) + NORM_EPS)
    yield
    ab = x[:, 1024:1152]
    z = ab + dtb
    lg_cols = -jnp.exp(alog) * (jnp.maximum(z, 0.0) + jnp.log1p(jnp.exp(-jnp.abs(z))))
    beta_cols = jax.nn.sigmoid(ab)
    r = lax.broadcasted_iota(jnp.int32, (128, 256), 0)
    hd = lax.broadcasted_iota(jnp.int32, (128, 256), 1) >> 6
    sel_g = jnp.where(r == direction * 8 + hd, 1.0, 0.0).astype(BF16)
    sel_b = jnp.where(r == direction * 8 + 4 + hd, 1.0, 0.0).astype(BF16)
    lg = _dot_sel(lg_cols, sel_g)
    beta = _dot_sel(beta_cols, sel_b)
    yield
    gc = sum(jnp.dot(masks["tri"], p, preferred_element_type=F32) for p in _split3(lg))
    last = 0 if reverse else c - 1
    glast = [gc[ci * c + last:ci * c + last + 1, :] for ci in range(nchunk)]
    gtot = jnp.concatenate([jnp.broadcast_to(g, (c, 256)) for g in glast], axis=0)
    eg = jnp.exp(gc)
    kb = k * beta
    env.update(q=q, kb=kb, kbf=k.astype(BF16), glast=glast,
               wu_rhs=jnp.concatenate([kb * eg, v * beta], axis=1).astype(BF16))
    yield
    env.update(ktail=k * jnp.exp(gtot - gc), qdec=q * eg)
    yield
    hi = gc.astype(BF16).astype(F32)
    r1 = gc - hi
    mid = r1.astype(BF16).astype(F32)
    lo = (r1 - mid).astype(BF16).astype(F32)
    l6 = masks["lane"] & 63
    env["ex"] = jnp.where(l6 == 0, hi, jnp.where(l6 == 1, mid, jnp.where(l6 == 2, lo, jnp.where(l6 < 6, 1.0, 0.0))))
    yield
    ey = jnp.where(l6 < 3, 1.0, jnp.where(l6 == 3, -hi, jnp.where(l6 == 4, -mid, jnp.where(l6 == 5, -lo, 0.0))))
    env["ey"] = ey.astype(BF16)
    yield


def _block_triangular_inverse(mats, ii, jj):
    same = lambda shift: (ii >> shift) == (jj >> shift)
    mm = lambda x, y: jnp.dot(x, y, preferred_element_type=F32)
    eye = jnp.where(ii == jj, 1.0, 0.0)
    a8 = [jnp.where(same(3), a, 0.0) for a in mats]
    a8b = [x.astype(BF16) for x in a8]
    p2 = [mm(x, x).astype(BF16) for x in a8b]
    t = [eye - x for x in a8]
    yield
    t = [x + mm(x.astype(BF16), p) for x, p in zip(t, p2)]
    yield
    p4 = [mm(p, p).astype(BF16) for p in p2]
    yield
    t = [x + mm(x.astype(BF16), p) for x, p in zip(t, p4)]
    yield
    for shift in (4, 5, 6):
        sel = same(shift) & jnp.logical_not(same(shift - 1))
        off = [jnp.where(sel, a, 0.0).astype(BF16) for a in mats]
        tb = [x.astype(BF16) for x in t]
        to = [mm(x, o).astype(BF16) for x, o in zip(tb, off)]
        yield
        t = [x - mm(y, xb) for x, y, xb in zip(t, to, tb)]
        yield
    return t


def _dn_row_solve(env, masks, *, emit):
    heads = range(DN_HEADS)
    hm, incl, strict = masks["hm"], masks["incl"], masks["strict"]
    nt_dims = (((1,), (1,)), ((), ()))
    e = [lax.dot_general(jnp.where(hm[h], env["ex"], 0.0).astype(BF16), env["ey"], nt_dims,
                         preferred_element_type=F32) for h in heads]
    yield
    decay = [jnp.exp(jnp.where(incl, x, NEG_BIG)) for x in e]
    kk = [_dot_nt(jnp.where(hm[h], env["kb"], 0.0), env["kbf"]) for h in heads]
    yield
    a = [jnp.where(strict, x * d, 0.0) for x, d in zip(kk, decay)]
    tinv = yield from _block_triangular_inverse(a, masks["ii"], masks["jj"])
    wu = [jnp.dot(t.astype(BF16), env["wu_rhs"], preferred_element_type=F32) for t in tinv]
    yield
    env["w"] = sum(jnp.where(hm[h], wu[h][:, 0:256], 0.0) for h in heads)
    env["u"] = sum(jnp.where(hm[h], wu[h][:, 256:512], 0.0) for h in heads)
    if emit:
        env["qk"] = [(_dot_nt(jnp.where(hm[h], env["q"], 0.0), env["kbf"]) * decay[h]).astype(BF16) for h in heads]
    yield


def _dn_row_scan(env, st, masks, *, reverse, emit):
    c = DN_CHUNK
    nchunk = env["w"].shape[0] // c
    vnew = [None] * nchunk
    ocross = [None] * nchunk
    for step in range(nchunk):
        ci = (nchunk - 1 - step) if reverse else step
        rows = slice(ci * c, (ci + 1) * c)
        s_old = st[...]
        sb = s_old.astype(BF16)
        vnew[ci] = env["u"][rows] - jnp.dot(env["w"][rows].astype(BF16), sb, preferred_element_type=F32)
        if emit:
            ocross[ci] = jnp.dot(env["qdec"][rows].astype(BF16), sb, preferred_element_type=F32)
        upd = _dot_tn(env["ktail"][rows], vnew[ci])
        st[...] = s_old * jnp.exp(env["glast"][ci]) + jnp.where(masks["bd"], upd, 0.0)
        yield
    if emit:
        vn = jnp.concatenate(vnew, axis=0).astype(BF16)
        intra = [jnp.dot(m, vn, preferred_element_type=F32) for m in env["qk"]]
        yield
        env["o"] = jnp.concatenate(ocross, axis=0) + sum(jnp.where(masks["hm"][h], intra[h], 0.0)
                                                         for h in range(DN_HEADS))
    yield


def _emit_interleaved(*gens):
    live = list(gens)
    while live:
        for g in list(live):
            try:
                next(g)
            except StopIteration:
                live.remove(g)


def _dn_kernel(*refs, reverse, nt, with_ctx_out, final):
    it = iter(refs)
    xc_ref, x_ref, xp_ref, xn_ref, cw_ref, alog_ref, dtb_ref = (next(it) for _ in range(7))
    if final:
        g_ref, of_ref = next(it), next(it)
        ofc_ref = next(it) if with_ctx_out else None
    o_ref = next(it)
    oc_ref = next(it) if with_ctx_out else None
    xext, st = next(it), next(it)
    s = pl.program_id(1)
    direction = 1 if reverse else 0
    nrow = x_ref.shape[0]
    alog, dtb = alog_ref[...], dtb_ref[...]

    def run(xs, prevs, nexts, emit):
        masks = _dn_masks(xs[0].shape[0], reverse)
        envs = [dict() for _ in range(nrow)]
        prep = [_dn_row_prep(envs[r], xs[r], prevs[r], nexts[r], cw_ref, alog, dtb, xext.at[r], masks,
                             direction=direction, reverse=reverse) for r in range(nrow)]
        solve = [_dn_row_solve(envs[r], masks, emit=emit) for r in range(nrow)]
        scan = [_dn_row_scan(envs[r], st.at[r], masks, reverse=reverse, emit=emit) for r in range(nrow)]
        _emit_interleaved(prep[0])
        for r in range(nrow):
            _emit_interleaved(solve[r], *([prep[r + 1]] if r + 1 < nrow else []), *([scan[r - 1]] if r else []))
        _emit_interleaved(scan[nrow - 1])
        return [env.get("o") for env in envs]

    def finish(os, xs, other_ref, out_ref):
        for r in range(nrow):
            o = os[r]
            if final:
                o = o + other_ref[r]
                ms = _group_sum(o * o, _group_ones(256, 6)) * (1.0 / DN_HEAD_DIM)
                o = o * lax.rsqrt(ms + NORM_EPS) * g_ref[...] * _silu(xs[r][:, 768:1024])
            out_ref[r] = o

    zeros8 = [jnp.zeros((8, 768), F32)] * nrow

    @pl.when(s == 0)
    def _():
        st[...] = jnp.zeros_like(st)
        xs = [xc_ref[r] for r in range(nrow)]
        os = run(xs, zeros8, zeros8, with_ctx_out)
        if with_ctx_out:
            finish(os, xs, ofc_ref if final else None, oc_ref)

    @pl.when(s > 0)
    def _():
        tile = (nt - s) if reverse else (s - 1)
        xs = [x_ref[r] for r in range(nrow)]
        prevs = [jnp.where(tile > 0, xp_ref[r], 0.0) for r in range(nrow)]
        nexts = [jnp.where(tile < nt - 1, xn_ref[r], 0.0) for r in range(nrow)]
        os = run(xs, prevs, nexts, True)
        finish(os, xs, of_ref if final else None, o_ref)


def _deltanet_direction(zb, zbc, conv_w, alog_v, dtb_v, norm_g, other, other_c, *, reverse, with_ctx_out):
    b, s, _ = zb.shape
    l = zbc.shape[1]
    tt = DN_TILE
    assert l == tt and s % tt == 0
    nt = s // tt
    nrow = DN_ROWS if b % DN_ROWS == 0 else 1
    final = other is not None
    r8 = tt // 8
    tile_of = (lambda st_: nt - 1 - jnp.maximum(st_ - 1, 0)) if reverse else (lambda st_: jnp.maximum(st_ - 1, 0))
    tok = lambda w_: pl.BlockSpec((nrow, tt, w_), lambda i, j: (i, tile_of(j), 0))
    ctx = lambda w_: pl.BlockSpec((nrow, l, w_), lambda i, j: (i, 0, 0))
    halo_p = pl.BlockSpec((nrow, 8, 768), lambda i, j: (i, jnp.maximum(tile_of(j) * r8 - 1, 0), 0))
    halo_n = pl.BlockSpec((nrow, 8, 768), lambda i, j: (i, jnp.minimum((tile_of(j) + 1) * r8, s // 8 - 1), 0))
    in_specs = [ctx(ZB_W), tok(ZB_W), halo_p, halo_n, _const_spec((8, 768)), _const_spec((1, 128)), _const_spec((1, 128))]
    args = [zbc, zb, zb, zb, conv_w, alog_v, dtb_v]
    if final:
        in_specs += [_const_spec((1, 256)), tok(256)]
        args += [norm_g, other]
        if with_ctx_out:
            in_specs.append(ctx(256))
            args.append(other_c)
    out_specs = [tok(256)]
    out_shape = [jax.ShapeDtypeStruct((b, s, 256), F32)]
    if with_ctx_out:
        out_specs.append(ctx(256))
        out_shape.append(jax.ShapeDtypeStruct((b, l, 256), F32))
    scratch = [pltpu.VMEM((nrow, tt + 16, 768), F32), pltpu.VMEM((nrow, 256, 256), F32)]
    outs = pl.pallas_call(
        functools.partial(_dn_kernel, reverse=reverse, nt=nt, with_ctx_out=with_ctx_out, final=final),
        name="deltanet_bwd" if reverse else "deltanet_fwd",
        grid=(b // nrow, nt + 1),
        in_specs=in_specs,
        out_specs=out_specs,
        out_shape=out_shape,
        scratch_shapes=scratch,
        compiler_params=_params("parallel", "arbitrary"),
    )(*args)
    return (outs[0], outs[1]) if with_ctx_out else (outs[0], None)


def _deltanet(zb, zbc, conv_w, a_log, dt_bias, norm_g, with_ctx_out):
    cw = jnp.pad(conv_w, ((0, 8 - DN_CONV), (0, 0)))
    cols = np.array([d * 8 + h for d in range(2) for h in range(DN_HEADS)])
    gate_vec = lambda p: jnp.zeros((1, 128), F32).at[0, cols].set(p.reshape(-1))
    alog_v, dtb_v = gate_vec(a_log), gate_vec(dt_bias)
    g = jnp.tile(norm_g, DN_HEADS).reshape(1, 256)
    o_f, oc_f = _deltanet_direction(zb, zbc, cw, alog_v, dtb_v, None, None, None,
                                    reverse=False, with_ctx_out=with_ctx_out)
    return _deltanet_direction(zb, zbc, cw, alog_v, dtb_v, g, o_f, oc_f, reverse=True, with_ctx_out=with_ctx_out)


def _ret_masks():
    lane_q = _lane_iota(128)
    lane_v = _lane_iota(256)
    rq = (lax.broadcasted_iota(jnp.int32, (128, 256), 0) >> 4) & 3
    cv = lax.broadcasted_iota(jnp.int32, (128, 256), 1) >> 6
    return (lane_q >> 4) & 3, lane_v >> 6, rq == cv


def _ret_ctx_state(kc, vc, lgq, bd, *, reverse):
    l = kc.shape[0]
    pos = lax.broadcasted_iota(jnp.int32, (l, 128), 0).astype(F32)
    wgt = pos if reverse else (l - 1.0 - pos)
    return jnp.where(bd, _dot_tn(kc * jnp.exp(wgt * lgq), vc), 0.0)


def _ret_state_update(r_old, k, v, lgq, lgv, bd, *, reverse):
    c = k.shape[0]
    pos = lax.broadcasted_iota(jnp.int32, (c, 128), 0).astype(F32)
    wgt = pos if reverse else (c - 1.0 - pos)
    return r_old * jnp.exp(c * lgv) + jnp.where(bd, _dot_tn(k * jnp.exp(wgt * lgq), v), 0.0)


def _ret_state_kernel(*refs, has_ctx):
    if has_ctx:
        k_ref, v_ref, kc_ref, vc_ref, l1q_ref, l1v_ref, o_ref, r_s = refs
    else:
        k_ref, v_ref, l1q_ref, l1v_ref, o_ref, r_s = refs
    _, _, bd = _ret_masks()
    lgq = jnp.log1p(-jnp.exp(l1q_ref[1:2, :]))
    lgv = jnp.log1p(-jnp.exp(l1v_ref[1:2, :]))

    @pl.when(pl.program_id(1) == 0)
    def _():
        if has_ctx:
            r_s[...] = _ret_ctx_state(kc_ref[0], vc_ref[0], lgq, bd, reverse=True)
        else:
            r_s[...] = jnp.zeros_like(r_s)

    r_old = r_s[...]
    o_ref[0, 0] = r_old
    r_s[...] = _ret_state_update(r_old, k_ref[0], v_ref[0], lgq, lgv, bd, reverse=True)


def _ret_main_kernel(*refs, has_ctx):
    if has_ctx:
        q_ref, k_ref, v_ref, g_ref, rb_ref, kc_ref, vc_ref, l1q_ref, l1v_ref, l1r_ref, ng_ref, o_ref, r_s = refs
    else:
        q_ref, k_ref, v_ref, g_ref, rb_ref, l1q_ref, l1v_ref, l1r_ref, ng_ref, o_ref, r_s = refs
    hq, hv, bd = _ret_masks()
    lgq = jnp.log1p(-jnp.exp(l1q_ref[...]))
    lgv = jnp.log1p(-jnp.exp(l1v_ref[...]))
    lgr = jnp.log1p(-jnp.exp(l1r_ref[...]))

    @pl.when(pl.program_id(1) == 0)
    def _():
        if has_ctx:
            r_s[...] = _ret_ctx_state(kc_ref[0], vc_ref[0], lgq[0:1], bd, reverse=False)
        else:
            r_s[...] = jnp.zeros_like(r_s)

    q = q_ref[0] * (RET_QK_DIM ** -0.5)
    k = k_ref[0]
    v = v_ref[0]
    c = q.shape[0]
    ii = lax.broadcasted_iota(jnp.int32, (c, c), 0)
    jj = lax.broadcasted_iota(jnp.int32, (c, c), 1)
    rel = (ii - jj).astype(F32)
    heads = range(RET_HEADS)
    kb = k.astype(BF16)
    vb = v.astype(BF16)
    sc = [_dot_nt(jnp.where(hq == h, q, 0.0), kb) for h in heads]
    d_f = [jnp.where(ii >= jj, jnp.exp(jnp.maximum(rel, 0.0) * lgr[h:h + 1, :]), 0.0) for h in heads]
    d_b = [jnp.where(jj >= ii, jnp.exp(jnp.maximum(-rel, 0.0) * lgr[4 + h:5 + h, :]), 0.0) for h in heads]
    pv = [jnp.dot((sc[h] * (d_f[h] + d_b[h])).astype(BF16), vb, preferred_element_type=F32) for h in heads]
    o = sum(jnp.where(hv == h, pv[h], 0.0) for h in heads)
    pos = lax.broadcasted_iota(jnp.int32, (c, 128), 0).astype(F32)
    r_old = r_s[...]
    o = o + _dot(q * jnp.exp((pos + 1.0) * lgq[0:1]), r_old) + _dot(q * jnp.exp((c - pos) * lgq[1:2]), rb_ref[0, 0])
    r_s[...] = _ret_state_update(r_old, k, v, lgq[0:1], lgv[0:1], bd, reverse=False)
    ones64 = _group_ones(256, 6)
    mu = _group_sum(o, ones64) * (1.0 / RET_V_DIM)
    d = o - mu
    var = _group_sum(d * d, ones64) * (1.0 / RET_V_DIM)
    o_ref[0] = d * lax.rsqrt(var + NORM_EPS) * ng_ref[...] * _silu(g_ref[0])


def _retention(zc, zcc, log1m_gamma, norm_g):
    b, t, _ = zc.shape
    c = min(RET_TILE, t)
    assert t % c == 0
    nc = t // c
    has_ctx = zcc is not None
    l1q = jnp.tile(jnp.repeat(log1m_gamma, 16, axis=-1), (1, 2))
    l1v = jnp.repeat(log1m_gamma, 64, axis=-1)
    l1r = jnp.broadcast_to(log1m_gamma.reshape(8, 1), (8, c))
    col = lambda cb, w_, f: pl.BlockSpec((1, c, w_), lambda i, n: (i, f(n), cb))
    fwd = lambda n: n
    bwd = lambda n: nc - 1 - n
    ctx_specs, ctx_args = [], []
    if has_ctx:
        l = zcc.shape[1]
        ctx_specs = [pl.BlockSpec((1, l, 128), lambda i, n: (i, 0, 1)), pl.BlockSpec((1, l, 256), lambda i, n: (i, 0, 1))]
        ctx_args = [zcc, zcc]
    rb = pl.pallas_call(
        functools.partial(_ret_state_kernel, has_ctx=has_ctx),
        name="retention_state",
        grid=(b, nc),
        in_specs=[col(1, 128, bwd), col(1, 256, bwd)] + ctx_specs + [_const_spec((2, 128)), _const_spec((2, 256))],
        out_specs=pl.BlockSpec((1, 1, 128, 256), lambda i, n: (i, bwd(n), 0, 0)),
        out_shape=jax.ShapeDtypeStruct((b, nc, 128, 256), F32),
        scratch_shapes=[pltpu.VMEM((128, 256), F32)],
        compiler_params=_params("parallel", "arbitrary"),
    )(zc, zc, *ctx_args, l1q, l1v)
    return pl.pallas_call(
        functools.partial(_ret_main_kernel, has_ctx=has_ctx),
        name="retention_main",
        grid=(b, nc),
        in_specs=[col(0, 128, fwd), col(1, 128, fwd), col(1, 256, fwd), col(2, 256, fwd),
                  pl.BlockSpec((1, 1, 128, 256), lambda i, n: (i, n, 0, 0))] + ctx_specs
                 + [_const_spec((2, 128)), _const_spec((2, 256)), _const_spec((8, c)), _const_spec((1, 256))],
        out_specs=pl.BlockSpec((1, c, 256), lambda i, n: (i, n, 0)),
        out_shape=jax.ShapeDtypeStruct((b, t, 256), F32),
        scratch_shapes=[pltpu.VMEM((128, 256), F32)],
        compiler_params=_params("parallel", "arbitrary"),
    )(zc, zc, zc, zc, rb, *ctx_args, l1q, l1v, l1r, norm_g.reshape(1, 256))


def _mla_weights(w_uq, w_ukv):
    dq = MLA_NOPE_DIM + MLA_ROPE_DIM
    qa = np.full((MLA_HEADS, 128), -1, np.int32)
    qb = np.full((MLA_HEADS, 128), -1, np.int32)
    kk = np.full((MLA_HEADS, 128), -1, np.int32)
    vv = np.zeros((MLA_HEADS, 64), np.int32)
    for h in range(MLA_HEADS):
        qa[h, 0:96] = h * dq + np.arange(96)
        qb[h, 64:80] = h * dq + 64 + 16 + np.arange(16)
        qb[h, 80:96] = h * dq + 64 + np.arange(16)
        kk[h, 0:64] = h * 128 + np.arange(64)
        vv[h] = h * 128 + 64 + np.arange(64)

    def take(w, idx):
        idx = idx.reshape(-1)
        out = jnp.take(w, jnp.asarray(np.maximum(idx, 0)), axis=1) * jnp.asarray((idx >= 0).astype(np.float32))[None, :]
        return out.astype(BF16)

    return take(w_uq, qa), take(w_uq, qb), take(w_ukv, kk), take(w_ukv, vv)


def _mla_prep_kernel(*refs, rope):
    if rope:
        zd_ref, qn_ref, kn_ref, wqa_ref, wqb_ref, wk_ref, wv_ref, cd_ref, sd_ref, q_ref, k_ref, v_ref = refs
    else:
        zd_ref, qn_ref, kn_ref, wqa_ref, wk_ref, wv_ref, q_ref, k_ref, v_ref = refs
    zd = zd_ref[0]
    cq, ckv, kr = zd[:, 0:256], zd[:, 256:384], zd[:, 384:512]
    nq = cq * lax.rsqrt(jnp.mean(cq * cq, axis=-1, keepdims=True) + NORM_EPS) * qn_ref[...]
    nkv = ckv * lax.rsqrt(jnp.mean(ckv * ckv, axis=-1, keepdims=True) + NORM_EPS) * kn_ref[...]
    scale = (MLA_NOPE_DIM + MLA_ROPE_DIM) ** -0.5 * LOG2_E
    qa = _dot(nq, wqa_ref[...])
    if rope:
        qb = _dot(nq, wqb_ref[...])
        cd, sd = cd_ref[...], sd_ref[...]
    kall = _dot(nkv, wk_ref[...])
    for h in range(MLA_HEADS):
        sl = slice(h * 128, (h + 1) * 128)
        qh = qa[:, sl]
        if rope:
            qh = qh * cd + qb[:, sl] * sd
        q_ref[0, :, sl] = (qh * scale).astype(BF16)
        k_ref[0, :, sl] = (kall[:, sl] + kr).astype(BF16)
    v_ref[0] = _dot(nkv, wv_ref[...]).astype(BF16)


def _mla_prep(zd, q_norm, kv_norm, weights, tables):
    b, t, _ = zd.shape
    tm = min(TOK_TILE, t)
    rope = tables is not None
    wqa, wqb, wk, wv = weights
    tok = lambda w_: pl.BlockSpec((1, tm, w_), lambda i, j: (i, j, 0))
    tab = pl.BlockSpec((tm, 128), lambda i, j: (j, 0))
    in_specs = [tok(ZD_W), _const_spec((1, 256)), _const_spec((1, 128)), _const_spec((256, 512))]
    args = [zd, q_norm.reshape(1, 256), kv_norm.reshape(1, 128), wqa]
    if rope:
        in_specs.append(_const_spec((256, 512)))
        args.append(wqb)
    in_specs += [_const_spec((128, 512)), _const_spec((128, 256))]
    args += [wk, wv]
    if rope:
        in_specs += [tab, tab]
        args += list(tables)
    return pl.pallas_call(
        functools.partial(_mla_prep_kernel, rope=rope),
        name="mla_prep",
        grid=(b, t // tm),
        in_specs=in_specs,
        out_specs=[tok(512), tok(512), tok(256)],
        out_shape=[jax.ShapeDtypeStruct((b, t, 512), BF16), jax.ShapeDtypeStruct((b, t, 512), BF16),
                   jax.ShapeDtypeStruct((b, t, 256), BF16)],
        compiler_params=_params("parallel", "parallel"),
    )(*args)


def _mla_attn_kernel(*refs, two_segments):
    scr = refs[-8:]
    s_scr, p_scr, m_scr, l_scr = scr[0:2], scr[2:4], scr[4:6], scr[6:8]
    if two_segments:
        q_ref, k1_ref, v1_ref, k2_ref, v2_ref, o_ref = refs[:-8]
        segments = [(k1_ref, v1_ref), (k2_ref, v2_ref)]
    else:
        q_ref, k2_ref, v2_ref, o_ref = refs[:-8]
        segments = [(k2_ref, v2_ref)]
    chunks = []
    off = 0
    for k_ref, v_ref in segments:
        n = k_ref.shape[1]
        step = min(MLA_KEY_CHUNK, n)
        for r0 in range(0, n, step):
            chunks.append((k_ref, v_ref, r0, step, off))
            off += step
    tq = q_ref.shape[1]
    lane = _lane_iota(256)
    nt_dims = (((1,), (1,)), ((), ()))

    def scores(h, ci):
        k_ref, _, r0, rows, col = chunks[ci]
        sl = slice(h * 128, (h + 1) * 128)
        s = lax.dot_general(q_ref[0, :, sl], k_ref[0, r0:r0 + rows, sl], nt_dims, preferred_element_type=F32)
        s_scr[h % 2][:, col:col + rows] = s
        cm = s[:, 0:128]
        for j in range(1, rows // 128):
            cm = jnp.maximum(cm, s[:, j * 128:(j + 1) * 128])
        m_ref = m_scr[h % 2]
        m_ref[...] = cm if ci == 0 else jnp.maximum(m_ref[...], cm)

    def row_max(h):
        m_ref = m_scr[h % 2]
        m_ref[...] = jnp.broadcast_to(jnp.max(m_ref[...], axis=-1, keepdims=True), (tq, 128))

    def probabilities(h, ci):
        _, _, _, rows, col = chunks[ci]
        s_ref, p_ref, m_ref, l_ref = s_scr[h % 2], p_scr[h % 2], m_scr[h % 2], l_scr[h % 2]
        for rb in range(0, tq, MLA_ROW_BLOCK):
            rsl = slice(rb, rb + MLA_ROW_BLOCK)
            mb = m_ref[rsl, :]
            lsum = None
            for j in range(rows // 128):
                csl = slice(col + j * 128, col + (j + 1) * 128)
                p = jnp.exp2(s_ref[rsl, csl] - mb)
                lsum = p if lsum is None else lsum + p
                p_ref[rsl, csl] = p.astype(BF16)
            l_ref[rsl, :] = lsum if ci == 0 else l_ref[rsl, :] + lsum

    def weighted_values(h):
        o = None
        col = 0
        for _, v_ref in segments:
            n = v_ref.shape[1]
            t = jnp.dot(p_scr[h % 2][:, col:col + n], v_ref[0], preferred_element_type=F32)
            o = t if o is None else o + t
            col += n
        den = jnp.sum(l_scr[h % 2][...], axis=-1, keepdims=True)
        return o * jnp.where((lane >> 6) == h, 1.0 / den, 0.0)

    for ci in range(len(chunks)):
        scores(0, ci)
    row_max(0)
    total = None
    for h in range(MLA_HEADS):
        for ci in range(len(chunks)):
            if h + 1 < MLA_HEADS:
                scores(h + 1, ci)
            probabilities(h, ci)
        if h + 1 < MLA_HEADS:
            row_max(h + 1)
        o = weighted_values(h)
        total = o if total is None else total + o
    o_ref[0] = total


def _mla_attend(q, kv_latent, kv_ctx):
    b, t, _ = q.shape
    tq = min(MLA_TQ, t)
    two = kv_latent is not None
    full = lambda n, w_: pl.BlockSpec((1, n, w_), lambda i, j: (i, 0, 0))
    in_specs = [pl.BlockSpec((1, tq, 512), lambda i, j: (i, j, 0))]
    args = [q]
    n_keys = 0
    for kv in ((kv_latent, kv_ctx) if two else (kv_ctx,)):
        n = kv[0].shape[1]
        assert n % min(MLA_KEY_CHUNK, n) == 0 and min(MLA_KEY_CHUNK, n) % 128 == 0
        n_keys += n
        in_specs += [full(n, 512), full(n, 256)]
        args += list(kv)
    return pl.pallas_call(
        functools.partial(_mla_attn_kernel, two_segments=two),
        name="mla_attention",
        grid=(b, t // tq),
        in_specs=in_specs,
        out_specs=pl.BlockSpec((1, tq, 256), lambda i, j: (i, j, 0)),
        out_shape=jax.ShapeDtypeStruct((b, t, 256), F32),
        scratch_shapes=[pltpu.VMEM((tq, n_keys), F32)] * 2 + [pltpu.VMEM((tq, n_keys), BF16)] * 2
                       + [pltpu.VMEM((tq, 128), F32)] * 4,
        compiler_params=_params("parallel", "arbitrary"),
    )(*args)


def _proj_mlp_kernel(ya_ref, yb_ref, yc_ref, yd_ref, x_ref, g1_ref, sc_ref, sh_ref, g2_ref, wo_ref, w1_ref, w2_ref,
                     l1g_ref, l1b_ref, l2g_ref, l2b_ref, o_ref):
    y = jnp.concatenate([ya_ref[0], yb_ref[0], yc_ref[0], yd_ref[0]], axis=-1).astype(BF16)
    proj = jnp.dot(y, wo_ref[...], preferred_element_type=F32)
    x1 = _layer_norm(DEEPNORM_ALPHA * x_ref[0] + g1_ref[0] * proj, l1g_ref[...], l1b_ref[...])
    h = (x1 * (1.0 + sc_ref[0]) + sh_ref[0]).astype(BF16)
    acc = None
    for j in range(D_FF // FF_CHUNK):
        cols = slice(j * FF_CHUNK, (j + 1) * FF_CHUNK)
        u = jnp.maximum(jnp.dot(h, w1_ref[:, cols], preferred_element_type=F32), 0.0)
        t = jnp.dot((u * u).astype(BF16), w2_ref[cols, :], preferred_element_type=F32)
        acc = t if acc is None else acc + t
    o_ref[0] = _layer_norm(DEEPNORM_ALPHA * x1 + g2_ref[0] * acc, l2g_ref[...], l2b_ref[...])


def _proj_mlp(ys, x, g1, sc2, sh2, g2, w_out, w1, w2, ln1_g, ln1_b, ln2_g, ln2_b):
    b, t, d = x.shape
    tm = min(TOK_TILE, t)
    tok = lambda w_: pl.BlockSpec((1, tm, w_), lambda i, j: (i, j, 0))
    vec = pl.BlockSpec((1, 1, d), lambda i, j: (i, 0, 0))
    row = lambda v: v.reshape(1, d)
    return pl.pallas_call(
        _proj_mlp_kernel,
        name="out_proj_mlp",
        grid=(b, t // tm),
        in_specs=[tok(256)] * 4 + [tok(d), vec, vec, vec, vec, _resident_spec((d, d)), _resident_spec((d, D_FF)),
                                   _resident_spec((D_FF, d))] + [_const_spec((1, d))] * 4,
        out_specs=tok(d),
        out_shape=jax.ShapeDtypeStruct((b, t, d), F32),
        compiler_params=_params("parallel", "parallel"),
    )(*ys, x, g1, sc2, sh2, g2, w_out, w1, w2, row(ln1_g), row(ln1_b), row(ln2_g), row(ln2_b))


def _rope_tables(n_tok):
    rows = n_tok // GRID_W
    freqs = lambda dim: ROPE_BASE ** (-jnp.arange(0, dim, 2, dtype=F32) / dim)
    row = jnp.broadcast_to(jnp.arange(rows, dtype=F32)[:, None], (rows, GRID_W)).reshape(-1)
    col = jnp.broadcast_to(jnp.arange(GRID_W, dtype=F32)[None, :], (rows, GRID_W)).reshape(-1)

    def axial(rot_dim):
        inv = freqs(rot_dim // 2)
        return jnp.concatenate([row[:, None] * inv, col[:, None] * inv], axis=-1)

    ang_a = axial(SWA_HEAD_DIM)
    ang_c = jnp.arange(n_tok, dtype=F32)[:, None] * freqs(RET_QK_DIM)
    ang_d = axial(MLA_ROPE_DIM)
    pair = lambda a, reps: (jnp.tile(jnp.cos(a), (1, 2 * reps)),
                            jnp.concatenate([-jnp.tile(jnp.sin(a), (1, reps)), jnp.tile(jnp.sin(a), (1, reps))], axis=-1))
    ca, sa = pair(ang_a, 2)
    cc, sc = pair(ang_c, 4)
    one = jnp.ones((n_tok, 64), F32)
    zero = jnp.zeros((n_tok, 64), F32)
    cd = jnp.concatenate([one, jnp.cos(ang_d), jnp.cos(ang_d), one[:, :32]], axis=-1)
    sd = jnp.concatenate([zero, -jnp.sin(ang_d), jnp.sin(ang_d), zero[:, :32]], axis=-1)
    return ca, sa, cc, sc, cd, sd


def kernel(x, c, ctx, c_ctx, ada_w, ada_b, w_in, swa_sink, dn_conv_w, dn_a_log, dn_dt_bias, dn_norm_g,
           ret_log1m_gamma, ret_norm_g, mla_q_norm, mla_w_uq, mla_kv_norm, mla_w_ukv, w_out, ln1_g, ln1_b,
           w_ff1, w_ff2, ln2_g, ln2_b):
    b, n_tok, d = x.shape
    tables = _rope_tables(n_tok)
    cc = jnp.concatenate([c, c_ctx[None, :], jnp.zeros((7, d), F32)], axis=0)
    perm_a = np.concatenate([np.arange(64) + 64 * hd for hd in (0, 2, 1, 3)])
    out_rows = np.concatenate([perm_a, np.arange(256, 1024)])
    xc = ctx
    for layer in range(DEPTH):
        with_ctx_out = layer < DEPTH - 1
        mod = _ada_modulation(cc, ada_w[layer], ada_b[layer])
        sh1, sc1, g1, sh2, sc2, g2 = jnp.split(mod[:b, None, :], 6, axis=-1)
        cmod = jnp.broadcast_to(mod[b][None, None, :], (b, 1, 6 * d))
        csh1, csc1, cg1, csh2, csc2, cg2 = jnp.split(cmod, 6, axis=-1)
        w_in_p = _permute_in_proj(w_in[layer])
        za, zb, zc, zd = _in_proj(x, sc1, sh1, w_in_p, tables)
        zac, zbc, zcc, zdc = _in_proj(xc, csc1, csh1, w_in_p, None)
        sink = swa_sink[layer]
        ya = _swa_local(za, zac, sink)
        yb, ybc = _deltanet(zb, zbc, dn_conv_w[layer], dn_a_log[layer], dn_dt_bias[layer], dn_norm_g[layer],
                            with_ctx_out)
        yr = _retention(zc, zcc, ret_log1m_gamma[layer], ret_norm_g[layer])
        mla_w = _mla_weights(mla_w_uq[layer], mla_w_ukv[layer])
        qd, kd, vd = _mla_prep(zd, mla_q_norm[layer], mla_kv_norm[layer], mla_w, tables[4:6])
        qdc, kdc, vdc = _mla_prep(zdc, mla_q_norm[layer], mla_kv_norm[layer], mla_w, None)
        yd = _mla_attend(qd, (kd, vd), (kdc, vdc))
        w_out_p = jnp.take(w_out[layer], jnp.asarray(out_rows), axis=0).astype(BF16)
        w1 = w_ff1[layer].astype(BF16)
        w2 = w_ff2[layer].astype(BF16)
        lns = (ln1_g[layer], ln1_b[layer], ln2_g[layer], ln2_b[layer])
        x_new = _proj_mlp((ya, yb, yr, yd), x, g1, sc2, sh2, g2, w_out_p, w1, w2, *lns)
        if with_ctx_out:
            yac = _swa_ctx(zac, sink)
            yrc = _retention(zcc, None, ret_log1m_gamma[layer], ret_norm_g[layer])
            ydc = _mla_attend(qdc, None, (kdc, vdc))
            xc = _proj_mlp((yac, ybc, yrc, ydc), xc, cg1, csc2, csh2, cg2, w_out_p, w1, w2, *lns)
        x = x_new
    return x
```

```python
import functools
import math

import numpy as np
import jax
import jax.numpy as jnp
from jax import lax
from jax.experimental import pallas as pl
from jax.experimental.pallas import tpu as pltpu

F32 = jnp.float32
BF16 = jnp.bfloat16

D_MODEL = 1024
DEPTH = 2
GRID_W = 64
SWA_HEADS, SWA_KV_HEADS, SWA_HEAD_DIM, SWA_WINDOW = 4, 2, 64, 128
DN_HEADS, DN_HEAD_DIM, DN_CONV, DN_CHUNK = 4, 64, 5, 64
RET_HEADS, RET_QK_DIM, RET_V_DIM = 4, 32, 64
MLA_HEADS, MLA_Q_RANK, MLA_KV_RANK, MLA_NOPE_DIM, MLA_ROPE_DIM, MLA_V_DIM = 4, 256, 128, 64, 32, 64
D_FF = 4 * D_MODEL
ROPE_BASE = 10000.0
NORM_EPS = 1e-6
LN_EPS = 1e-5
DEEPNORM_ALPHA = (2 * DEPTH) ** 0.25
IN_SPLITS = (256, 128, 128, 768, 256, 16, 128, 128, 256, 256, 256, 128, 32)

ZA_W, ZB_W, ZC_W, ZD_W, KRS_W = 512, 1152, 768, 512, 128
Z_OFF = np.cumsum((0, ZA_W, ZB_W, ZC_W, ZD_W, KRS_W))
Z_W = int(Z_OFF[-1])

VMEM_LIMIT = 48 * 1024 * 1024
NEG_BIG = -1e30

SWA_QTILE = 256
DN_TILE = 256
DN_ROWS = 4
RET_TILE = 256
MLA_TQ = 256
MLA_KEY_CHUNK = 512
MLA_ROW_BLOCK = 32
LOG2_E = 1.4426950408889634
TOK_TILE = 512
FF_CHUNK = 1024


def _dot(a, b):
    return jnp.dot(a.astype(BF16), b.astype(BF16), preferred_element_type=F32)


def _dot_nt(a, b):
    return lax.dot_general(a.astype(BF16), b.astype(BF16), (((1,), (1,)), ((), ())), preferred_element_type=F32)


def _dot_tn(a, b):
    return lax.dot_general(a.astype(BF16), b.astype(BF16), (((0,), (0,)), ((), ())), preferred_element_type=F32)


def _split2(a):
    hi = a.astype(BF16)
    lo = (a - hi.astype(F32)).astype(BF16)
    return hi, lo


def _split3(a):
    hi = a.astype(BF16)
    r = a - hi.astype(F32)
    mid = r.astype(BF16)
    lo = (r - mid.astype(F32)).astype(BF16)
    return hi, mid, lo


def _dot_sel(a, sel):
    return sum(jnp.dot(p, sel, preferred_element_type=F32) for p in _split3(a))


def _dot_sel_nt(sel, a):
    dn = (((1,), (1,)), ((), ()))
    return sum(lax.dot_general(sel, p, dn, preferred_element_type=F32) for p in _split3(a))


def _dot_hp(a, b):
    ah, al = _split2(a)
    bh, bl = _split2(b)
    f = lambda x, y: jnp.dot(x, y, preferred_element_type=F32)
    return f(ah, bh) + (f(ah, bl) + f(al, bh))


def _group_ones(width, shift):
    r = lax.broadcasted_iota(jnp.int32, (width, width), 0) >> shift
    c = lax.broadcasted_iota(jnp.int32, (width, width), 1) >> shift
    return jnp.where(r == c, 1.0, 0.0).astype(BF16)


def _group_sum(x, ones_bd):
    hi, lo = _split2(x)
    return jnp.dot(hi, ones_bd, preferred_element_type=F32) + jnp.dot(lo, ones_bd, preferred_element_type=F32)


def _silu(x):
    return x * jax.nn.sigmoid(x)


def _layer_norm(v, g, b):
    mu = jnp.mean(v, axis=-1, keepdims=True)
    d = v - mu
    var = jnp.mean(d * d, axis=-1, keepdims=True)
    return d * lax.rsqrt(var + LN_EPS) * g + b


def _lane_iota(width):
    return lax.broadcasted_iota(jnp.int32, (1, width), 1)


def _params(*sem):
    return pltpu.CompilerParams(dimension_semantics=sem, vmem_limit_bytes=VMEM_LIMIT)


def _const_spec(shape):
    nd = len(shape)
    return pl.BlockSpec(shape, lambda *_: (0,) * nd)


def _resident_spec(shape):
    nd = len(shape)
    return pl.BlockSpec(shape, lambda *_: (0,) * nd, pipeline_mode=pl.Buffered(1))


def _ada_kernel(c_ref, w_ref, b_ref, o_ref):
    o_ref[...] = _dot(_silu(c_ref[...]), w_ref[...]) + b_ref[...]


def _ada_modulation(cc, w, b):
    rows, d = cc.shape
    n = w.shape[1]
    tn = 1536
    return pl.pallas_call(
        _ada_kernel,
        name="ada_modulation",
        grid=(n // tn,),
        in_specs=[pl.BlockSpec((rows, d), lambda j: (0, 0)), pl.BlockSpec((d, tn), lambda j: (0, j)),
                  pl.BlockSpec((1, tn), lambda j: (0, j))],
        out_specs=pl.BlockSpec((rows, tn), lambda j: (0, j)),
        out_shape=jax.ShapeDtypeStruct((rows, n), F32),
        compiler_params=_params("parallel"),
    )(cc, w, b.reshape(1, n))


def _in_proj_columns():
    off = np.cumsum((0,) + IN_SPLITS)
    aq, ak, av, bqkv, bz, bab, cq, ck, cv, cg, dcq, dckv, dkr = (int(o) for o in off[:-1])
    ar = lambda base, n: list(range(base, base + n))
    cols = []
    for g in range(2):
        for half in range(2):
            for h in range(2):
                cols += ar(aq + (2 * h + g) * 64 + half * 32, 32)
    for half in range(2):
        for h in range(2):
            cols += ar(ak + h * 64 + half * 32, 32)
    cols += ar(av, 128)
    cols += ar(bqkv, 768) + ar(bz, 256) + ar(bab, 16) + [-1] * 112
    for base in (cq, ck):
        for half in range(2):
            for h in range(4):
                cols += ar(base + h * 32 + half * 16, 16)
    cols += ar(cv, 256) + ar(cg, 256)
    cols += ar(dcq, 256) + ar(dckv, 128)
    cols += [-1] * 64 + ar(dkr, 32) + [-1] * 32
    cols += [-1] * 64 + ar(dkr + 16, 16) + ar(dkr, 16) + [-1] * 32
    cols = np.asarray(cols, np.int32)
    assert cols.shape[0] == Z_W
    return cols


_IN_COLS = _in_proj_columns()


def _permute_in_proj(w_in):
    idx = jnp.asarray(np.maximum(_IN_COLS, 0))
    keep = jnp.asarray((_IN_COLS >= 0).astype(np.float32))
    return (jnp.take(w_in, idx, axis=1) * keep[None, :]).astype(BF16)


def _in_proj_kernel(*refs, rope):
    if rope:
        x_ref, sc_ref, sh_ref, w_ref, ca_ref, sa_ref, cc_ref, sc2_ref, cd_ref, sd_ref, oa, ob, oc, od = refs
    else:
        x_ref, sc_ref, sh_ref, w_ref, oa, ob, oc, od = refs
    h = x_ref[0] * (1.0 + sc_ref[0]) + sh_ref[0]
    z = _dot(h, w_ref[...])
    za = z[:, Z_OFF[0]:Z_OFF[1]]
    zc = z[:, Z_OFF[2]:Z_OFF[3]]
    zd = z[:, Z_OFF[3]:Z_OFF[4]]
    ob[0] = z[:, Z_OFF[1]:Z_OFF[2]]
    if not rope:
        oa[0] = za
        oc[0] = zc
        od[0] = zd
        return
    krs = z[:, Z_OFF[4]:Z_OFF[5]]

    def rot(t, c, s):
        return t * c + pltpu.roll(t, 64, 1) * s

    ca, sa = ca_ref[...], sa_ref[...]
    for j in range(3):
        oa[0, :, j * 128:(j + 1) * 128] = rot(za[:, j * 128:(j + 1) * 128], ca, sa)
    oa[0, :, 384:512] = za[:, 384:512]
    cc, sc2 = cc_ref[...], sc2_ref[...]
    for j in range(2):
        oc[0, :, j * 128:(j + 1) * 128] = rot(zc[:, j * 128:(j + 1) * 128], cc, sc2)
    oc[0, :, 256:768] = zc[:, 256:768]
    od[0, :, 0:384] = zd[:, 0:384]
    od[0, :, 384:512] = zd[:, 384:512] * cd_ref[...] + krs * sd_ref[...]


def _in_proj(x, sc, sh, w, tables):
    b, t, d = x.shape
    tm = min(TOK_TILE, t)
    rope = tables is not None
    tok = lambda w_: pl.BlockSpec((1, tm, w_), lambda i, j: (i, j, 0))
    vec = pl.BlockSpec((1, 1, d), lambda i, j: (i, 0, 0))
    tab = pl.BlockSpec((tm, 128), lambda i, j: (j, 0))
    in_specs = [tok(d), vec, vec, _resident_spec((d, Z_W))] + ([tab] * 6 if rope else [])
    args = [x, sc, sh, w] + (list(tables) if rope else [])
    widths = (ZA_W, ZB_W, ZC_W, ZD_W)
    return pl.pallas_call(
        functools.partial(_in_proj_kernel, rope=rope),
        name="in_proj_rope" if rope else "in_proj_ctx",
        grid=(b, t // tm),
        in_specs=in_specs,
        out_specs=[tok(w_) for w_ in widths],
        out_shape=[jax.ShapeDtypeStruct((b, t, w_), F32) for w_ in widths],
        compiler_params=_params("parallel", "parallel"),
    )(*args)


def _swa_heads(sink_ref, qg, keys, vals, mask):
    lane = _lane_iota(128)
    scale = SWA_HEAD_DIM ** -0.5
    kb = keys.astype(BF16)
    vb = vals.astype(BF16)
    hg = [(h, g) for g in range(2) for h in range(2)]
    s = [_dot_nt(jnp.where(((lane >> 5) & 1) == h, qg[g] * scale, 0.0), kb) for h, g in hg]
    if mask is not None:
        s = [jnp.where(mask, x, NEG_BIG) for x in s]
    snk = [sink_ref[2 * h + g] for h, g in hg]
    m = [jnp.maximum(jnp.max(x, axis=-1, keepdims=True), k_) for x, k_ in zip(s, snk)]
    p = [jnp.exp(x - y) for x, y in zip(s, m)]
    den = [jnp.sum(x, axis=-1, keepdims=True) + jnp.exp(k_ - y) for x, k_, y in zip(p, snk, m)]
    o = [jnp.dot(x.astype(BF16), vb, preferred_element_type=F32) for x in p]
    o = [x * jnp.where((lane >> 6) == h, 1.0 / d, 0.0) for x, d, (h, g) in zip(o, den, hg)]
    return jnp.concatenate([o[0] + o[1], o[2] + o[3]], axis=-1)


def _swa_local_kernel(sink_ref, q_ref, kp_ref, k0_ref, kn_ref, vp_ref, v0_ref, vn_ref, kc_ref, vc_ref, o_ref, *, nb):
    n = pl.program_id(1)
    w = SWA_WINDOW
    keys = jnp.concatenate([kp_ref[0], k0_ref[0], kn_ref[0], kc_ref[0]], axis=0)
    vals = jnp.concatenate([vp_ref[0], v0_ref[0], vn_ref[0], vc_ref[0]], axis=0)
    nk = keys.shape[0]
    qt = q_ref.shape[1]
    i = lax.broadcasted_iota(jnp.int32, (qt, nk), 0)
    j = lax.broadcasted_iota(jnp.int32, (qt, nk), 1)
    lo = jnp.where(n > 0, 0, w)
    hi = jnp.where(n < nb - 1, qt + 2 * w, qt + w)
    band = (j >= i) & (j <= i + 2 * w) & (j >= lo) & (j < hi)
    mask = band | (j >= qt + 2 * w)
    q = q_ref[0]
    o_ref[0] = _swa_heads(sink_ref, (q[:, :128], q[:, 128:]), keys, vals, mask)


def _swa_ctx_kernel(sink_ref, q_ref, kc_ref, vc_ref, o_ref):
    q = q_ref[0]
    o_ref[0] = _swa_heads(sink_ref, (q[:, :128], q[:, 128:]), kc_ref[0], vc_ref[0], None)


_SMEM_SPEC = pl.BlockSpec(memory_space=pltpu.SMEM)


def _swa_local(za, zac, sink):
    b, s, _ = za.shape
    l = zac.shape[1]
    w = SWA_WINDOW
    qt = min(SWA_QTILE, s)
    r = qt // w
    nb = s // qt
    blk = lambda col, f: pl.BlockSpec((1, w, 128), lambda i, n: (i, f(n), col))
    prev = lambda n: jnp.maximum(n * r - 1, 0)
    nxt = lambda n: jnp.minimum((n + 1) * r, s // w - 1)
    own = lambda col: pl.BlockSpec((1, qt, 128), lambda i, n: (i, n, col))
    ctx = lambda col: pl.BlockSpec((1, l, 128), lambda i, n: (i, 0, col))
    return pl.pallas_call(
        functools.partial(_swa_local_kernel, nb=nb),
        name="swa_local",
        grid=(b, nb),
        in_specs=[_SMEM_SPEC, pl.BlockSpec((1, qt, 256), lambda i, n: (i, n, 0)),
                  blk(2, prev), own(2), blk(2, nxt), blk(3, prev), own(3), blk(3, nxt), ctx(2), ctx(3)],
        out_specs=pl.BlockSpec((1, qt, 256), lambda i, n: (i, n, 0)),
        out_shape=jax.ShapeDtypeStruct((b, s, 256), F32),
        compiler_params=_params("parallel", "parallel"),
    )(sink, za, za, za, za, za, za, za, zac, zac)


def _swa_ctx(zac, sink):
    b, l, _ = zac.shape
    ctx = lambda col, w_: pl.BlockSpec((1, l, w_), lambda i: (i, 0, col))
    return pl.pallas_call(
        _swa_ctx_kernel,
        name="swa_ctx",
        grid=(b,),
        in_specs=[_SMEM_SPEC, ctx(0, 256), ctx(2, 128), ctx(3, 128)],
        out_specs=pl.BlockSpec((1, l, 256), lambda i: (i, 0, 0)),
        out_shape=jax.ShapeDtypeStruct((b, l, 256), F32),
        compiler_params=_params("parallel"),
    )(sink, zac, zac, zac)


def _dn_masks(tt, reverse):
    ii = lax.broadcasted_iota(jnp.int32, (tt, tt), 0)
    jj = lax.broadcasted_iota(jnp.int32, (tt, tt), 1)
    chunk = (ii >> 6) == (jj >> 6)
    incl = chunk & ((jj >= ii) if reverse else (jj <= ii))
    strict = chunk & ((jj > ii) if reverse else (jj < ii))
    lane = _lane_iota(256)
    rr = lax.broadcasted_iota(jnp.int32, (256, 256), 0) >> 6
    cc = lax.broadcasted_iota(jnp.int32, (256, 256), 1) >> 6
    same = lambda shift: (ii >> shift) == (jj >> shift)
    one_if = lambda cond: jnp.where(cond, 1.0, 0.0)
    hm = [(lane >> 6) == h for h in range(DN_HEADS)]
    return dict(incl=incl, tri=one_if(incl).astype(BF16), lane=lane, hm=hm, bd=rr == cc, eye=one_if(ii == jj),
                hm_b=[one_if(m).astype(BF16) for m in hm], diag8_f=one_if(same(3) & strict),
                diag8_b=one_if(same(3) & strict).astype(BF16),
                off_b=[one_if(same(sh) & jnp.logical_not(same(sh - 1)) & strict).astype(BF16) for sh in (4, 5, 6)])


def _dn_row_prep(env, x, prev8, next8, cw_ref, alog, dtb, xext, masks, *, direction, reverse):
    t = x.shape[0]
    c = DN_CHUNK
    nchunk = t // c
    xext[0:8, :] = prev8
    xext[8:8 + t, :] = x[:, 0:768]
    xext[8 + t:16 + t, :] = next8
    pad = DN_CONV // 2
    y = None
    for kk in range(DN_CONV):
        term = cw_ref[kk:kk + 1, :] * xext[pl.ds(8 - pad + kk, t), :]
        y = term if y is None else y + term
    yield
    y = _silu(y)
    yield
    ones64 = _group_ones(256, 6)
    q, k, v = y[:, 0:256], y[:, 256:512], y[:, 512:768]
    q = q * lax.rsqrt(_group_sum(q * q, ones64) + NORM_EPS) * (DN_HEAD_DIM ** -0.5)
    k = k * lax.rsqrt(_group_sum(k * k, ones64) + NORM_EPS)
    yield
    ab = x[:, 1024:1152]
    z = ab + dtb
    lg_cols = -jnp.exp(alog) * (jnp.maximum(z, 0.0) + jnp.log1p(jnp.exp(-jnp.abs(z))))
    beta_cols = jax.nn.sigmoid(ab)
    r = lax.broadcasted_iota(jnp.int32, (128, 256), 0)
    hd = lax.broadcasted_iota(jnp.int32, (128, 256), 1) >> 6
    sel_g = jnp.where(r == direction * 8 + hd, 1.0, 0.0).astype(BF16)
    sel_b = jnp.where(r == direction * 8 + 4 + hd, 1.0, 0.0).astype(BF16)
    lg = _dot_sel(lg_cols, sel_g)
    beta = _dot_sel(beta_cols, sel_b)
    yield
    gc = sum(jnp.dot(masks["tri"], p, preferred_element_type=F32) for p in _split3(lg))
    last = 0 if reverse else c - 1
    glast = [gc[ci * c + last:ci * c + last + 1, :] for ci in range(nchunk)]
    gtot = jnp.concatenate([jnp.broadcast_to(g, (c, 256)) for g in glast], axis=0)
    eg = jnp.exp(gc)
    kb = k * beta
    env.update(q_b=q.astype(BF16), kb_b=kb.astype(BF16), k_b=k.astype(BF16), glast=glast,
               wu_rhs=jnp.concatenate([kb * eg, v * beta], axis=1).astype(BF16))
    yield
    env.update(ktail=k * jnp.exp(gtot - gc), qdec=q * eg)
    yield
    hi = gc.astype(BF16).astype(F32)
    r1 = gc - hi
    mid = r1.astype(BF16).astype(F32)
    lo = (r1 - mid).astype(BF16).astype(F32)
    l6 = masks["lane"] & 63
    ex = jnp.where(l6 == 0, hi, jnp.where(l6 == 1, mid, jnp.where(l6 == 2, lo, jnp.where(l6 < 6, 1.0, 0.0))))
    env["ex"] = ex.astype(BF16)
    yield
    ey = jnp.where(l6 < 3, 1.0, jnp.where(l6 == 3, -hi, jnp.where(l6 == 4, -mid, jnp.where(l6 == 5, -lo, 0.0))))
    env["ey"] = ey.astype(BF16)
    yield


def _block_triangular_inverse(mats, masks):
    mm = lambda x, y: jnp.dot(x, y, preferred_element_type=F32)
    mats_b = [x.astype(BF16) for x in mats]
    a8b = [x * masks["diag8_b"] for x in mats_b]
    p2 = [mm(x, x).astype(BF16) for x in a8b]
    t = [masks["eye"] - x * masks["diag8_f"] for x in mats]
    yield
    t = [x + mm(x.astype(BF16), p) for x, p in zip(t, p2)]
    yield
    p4 = [mm(p, p).astype(BF16) for p in p2]
    yield
    t = [x + mm(x.astype(BF16), p) for x, p in zip(t, p4)]
    yield
    for level in range(3):
        off = [x * masks["off_b"][level] for x in mats_b]
        tb = [x.astype(BF16) for x in t]
        to = [mm(x, o).astype(BF16) for x, o in zip(tb, off)]
        yield
        t = [x - mm(y, xb) for x, y, xb in zip(t, to, tb)]
        yield
    return t


def _select_heads(hm, parts):
    out = parts[-1]
    for h in range(len(parts) - 2, -1, -1):
        out = jnp.where(hm[h], parts[h], out)
    return out


def _dn_row_solve(env, masks, *, emit):
    heads = range(DN_HEADS)
    hm, hm_b, incl = masks["hm"], masks["hm_b"], masks["incl"]
    nt_dims = (((1,), (1,)), ((), ()))
    mm_nt = lambda x, y: lax.dot_general(x, y, nt_dims, preferred_element_type=F32)
    e = [mm_nt(env["ex"] * hm_b[h], env["ey"]) for h in heads]
    yield
    decay = [jnp.exp(jnp.where(incl, x, NEG_BIG)) for x in e]
    kk = [mm_nt(env["kb_b"] * hm_b[h], env["k_b"]) for h in heads]
    yield
    tinv = yield from _block_triangular_inverse([x * d for x, d in zip(kk, decay)], masks)
    wu = [jnp.dot(t.astype(BF16), env["wu_rhs"], preferred_element_type=F32) for t in tinv]
    yield
    env["w"] = _select_heads(hm, [x[:, 0:256] for x in wu])
    env["u"] = _select_heads(hm, [x[:, 256:512] for x in wu])
    if emit:
        env["qk"] = [(mm_nt(env["q_b"] * hm_b[h], env["k_b"]) * decay[h]).astype(BF16) for h in heads]
    yield


def _dn_row_scan(env, st, masks, *, reverse, emit):
    c = DN_CHUNK
    nchunk = env["w"].shape[0] // c
    vnew = [None] * nchunk
    ocross = [None] * nchunk
    for step in range(nchunk):
        ci = (nchunk - 1 - step) if reverse else step
        rows = slice(ci * c, (ci + 1) * c)
        s_old = st[...]
        sb = s_old.astype(BF16)
        vnew[ci] = env["u"][rows] - jnp.dot(env["w"][rows].astype(BF16), sb, preferred_element_type=F32)
        if emit:
            ocross[ci] = jnp.dot(env["qdec"][rows].astype(BF16), sb, preferred_element_type=F32)
        upd = _dot_tn(env["ktail"][rows], vnew[ci])
        st[...] = s_old * jnp.exp(env["glast"][ci]) + jnp.where(masks["bd"], upd, 0.0)
        yield
    if emit:
        vn = jnp.concatenate(vnew, axis=0).astype(BF16)
        intra = [jnp.dot(m, vn, preferred_element_type=F32) for m in env["qk"]]
        yield
        env["o"] = jnp.concatenate(ocross, axis=0) + _select_heads(masks["hm"], intra)
    yield


def _emit_interleaved(*gens):
    live = list(gens)
    while live:
        for g in list(live):
            try:
                next(g)
            except StopIteration:
                live.remove(g)


def _dn_kernel(*refs, reverse, nt, with_ctx_out, final):
    it = iter(refs)
    xc_ref, x_ref, xp_ref, xn_ref, cw_ref, alog_ref, dtb_ref = (next(it) for _ in range(7))
    if final:
        g_ref, of_ref = next(it), next(it)
        ofc_ref = next(it) if with_ctx_out else None
    o_ref = next(it)
    oc_ref = next(it) if with_ctx_out else None
    xext, st = next(it), next(it)
    s = pl.program_id(1)
    direction = 1 if reverse else 0
    nrow = x_ref.shape[0]
    alog, dtb = alog_ref[...], dtb_ref[...]

    def run(xs, prevs, nexts, emit):
        masks = _dn_masks(xs[0].shape[0], reverse)
        envs = [dict() for _ in range(nrow)]
        prep = [_dn_row_prep(envs[r], xs[r], prevs[r], nexts[r], cw_ref, alog, dtb, xext.at[r], masks,
                             direction=direction, reverse=reverse) for r in range(nrow)]
        solve = [_dn_row_solve(envs[r], masks, emit=emit) for r in range(nrow)]
        scan = [_dn_row_scan(envs[r], st.at[r], masks, reverse=reverse, emit=emit) for r in range(nrow)]
        _emit_interleaved(prep[0])
        for r in range(nrow):
            _emit_interleaved(solve[r], *([prep[r + 1]] if r + 1 < nrow else []), *([scan[r - 1]] if r else []))
        _emit_interleaved(scan[nrow - 1])
        return [env.get("o") for env in envs]

    is_ctx = s == 0
    tile = (nt - s) if reverse else (s - 1)

    @pl.when(is_ctx)
    def _():
        st[...] = jnp.zeros_like(st)

    xs = [jnp.where(is_ctx, xc_ref[r], x_ref[r]) for r in range(nrow)]
    prevs = [jnp.where(jnp.logical_and(tile > 0, s > 0), xp_ref[r], 0.0) for r in range(nrow)]
    nexts = [jnp.where(jnp.logical_and(tile < nt - 1, s > 0), xn_ref[r], 0.0) for r in range(nrow)]
    os = run(xs, prevs, nexts, True)
    for r in range(nrow):
        o = os[r]
        if final:
            other = jnp.where(is_ctx, ofc_ref[r], of_ref[r]) if with_ctx_out else of_ref[r]
            o = o + other
            ms = _group_sum(o * o, _group_ones(256, 6)) * (1.0 / DN_HEAD_DIM)
            o = o * lax.rsqrt(ms + NORM_EPS) * g_ref[...] * _silu(xs[r][:, 768:1024])
        os[r] = o

    @pl.when(s > 0)
    def _():
        for r in range(nrow):
            o_ref[r] = os[r]

    if with_ctx_out:
        @pl.when(is_ctx)
        def _():
            for r in range(nrow):
                oc_ref[r] = os[r]


def _deltanet_direction(zb, zbc, conv_w, alog_v, dtb_v, norm_g, other, other_c, *, reverse, with_ctx_out):
    b, s, _ = zb.shape
    l = zbc.shape[1]
    tt = DN_TILE
    assert l == tt and s % tt == 0
    nt = s // tt
    nrow = DN_ROWS if b % DN_ROWS == 0 else 1
    final = other is not None
    r8 = tt // 8
    tile_of = (lambda st_: nt - 1 - jnp.maximum(st_ - 1, 0)) if reverse else (lambda st_: jnp.maximum(st_ - 1, 0))
    tok = lambda w_: pl.BlockSpec((nrow, tt, w_), lambda i, j: (i, tile_of(j), 0))
    ctx = lambda w_: pl.BlockSpec((nrow, l, w_), lambda i, j: (i, 0, 0))
    halo_p = pl.BlockSpec((nrow, 8, 768), lambda i, j: (i, jnp.maximum(tile_of(j) * r8 - 1, 0), 0))
    halo_n = pl.BlockSpec((nrow, 8, 768), lambda i, j: (i, jnp.minimum((tile_of(j) + 1) * r8, s // 8 - 1), 0))
    in_specs = [ctx(ZB_W), tok(ZB_W), halo_p, halo_n, _const_spec((8, 768)), _const_spec((1, 128)), _const_spec((1, 128))]
    args = [zbc, zb, zb, zb, conv_w, alog_v, dtb_v]
    if final:
        in_specs += [_const_spec((1, 256)), tok(256)]
        args += [norm_g, other]
        if with_ctx_out:
            in_specs.append(ctx(256))
            args.append(other_c)
    out_specs = [tok(256)]
    out_shape = [jax.ShapeDtypeStruct((b, s, 256), F32)]
    if with_ctx_out:
        out_specs.append(ctx(256))
        out_shape.append(jax.ShapeDtypeStruct((b, l, 256), F32))
    scratch = [pltpu.VMEM((nrow, tt + 16, 768), F32), pltpu.VMEM((nrow, 256, 256), F32)]
    outs = pl.pallas_call(
        functools.partial(_dn_kernel, reverse=reverse, nt=nt, with_ctx_out=with_ctx_out, final=final),
        name="deltanet_bwd" if reverse else "deltanet_fwd",
        grid=(b // nrow, nt + 1),
        in_specs=in_specs,
        out_specs=out_specs,
        out_shape=out_shape,
        scratch_shapes=scratch,
        compiler_params=_params("parallel", "arbitrary"),
    )(*args)
    return (outs[0], outs[1]) if with_ctx_out else (outs[0], None)


def _deltanet(zb, zbc, conv_w, a_log, dt_bias, norm_g, with_ctx_out):
    cw = jnp.pad(conv_w, ((0, 8 - DN_CONV), (0, 0)))
    cols = np.array([d * 8 + h for d in range(2) for h in range(DN_HEADS)])
    gate_vec = lambda p: jnp.zeros((1, 128), F32).at[0, cols].set(p.reshape(-1))
    alog_v, dtb_v = gate_vec(a_log), gate_vec(dt_bias)
    g = jnp.tile(norm_g, DN_HEADS).reshape(1, 256)
    o_f, oc_f = _deltanet_direction(zb, zbc, cw, alog_v, dtb_v, None, None, None,
                                    reverse=False, with_ctx_out=with_ctx_out)
    return _deltanet_direction(zb, zbc, cw, alog_v, dtb_v, g, o_f, oc_f, reverse=True, with_ctx_out=with_ctx_out)


def _ret_masks():
    lane_q = _lane_iota(128)
    lane_v = _lane_iota(256)
    rq = (lax.broadcasted_iota(jnp.int32, (128, 256), 0) >> 4) & 3
    cv = lax.broadcasted_iota(jnp.int32, (128, 256), 1) >> 6
    return (lane_q >> 4) & 3, lane_v >> 6, rq == cv


def _ret_ctx_state(kc, vc, lgq, bd, *, reverse):
    l = kc.shape[0]
    pos = lax.broadcasted_iota(jnp.int32, (l, 128), 0).astype(F32)
    wgt = pos if reverse else (l - 1.0 - pos)
    return jnp.where(bd, _dot_tn(kc * jnp.exp(wgt * lgq), vc), 0.0)


def _ret_state_update(r_old, k, v, lgq, lgv, bd, *, reverse):
    c = k.shape[0]
    pos = lax.broadcasted_iota(jnp.int32, (c, 128), 0).astype(F32)
    wgt = pos if reverse else (c - 1.0 - pos)
    return r_old * jnp.exp(c * lgv) + jnp.where(bd, _dot_tn(k * jnp.exp(wgt * lgq), v), 0.0)


def _ret_state_kernel(*refs, has_ctx):
    if has_ctx:
        k_ref, v_ref, kc_ref, vc_ref, l1q_ref, l1v_ref, o_ref, r_s = refs
    else:
        k_ref, v_ref, l1q_ref, l1v_ref, o_ref, r_s = refs
    _, _, bd = _ret_masks()
    lgq = jnp.log1p(-jnp.exp(l1q_ref[1:2, :]))
    lgv = jnp.log1p(-jnp.exp(l1v_ref[1:2, :]))

    @pl.when(pl.program_id(1) == 0)
    def _():
        if has_ctx:
            r_s[...] = _ret_ctx_state(kc_ref[0], vc_ref[0], lgq, bd, reverse=True)
        else:
            r_s[...] = jnp.zeros_like(r_s)

    r_old = r_s[...]
    o_ref[0, 0] = r_old
    r_s[...] = _ret_state_update(r_old, k_ref[0], v_ref[0], lgq, lgv, bd, reverse=True)


def _ret_main_kernel(*refs, has_ctx):
    if has_ctx:
        q_ref, k_ref, v_ref, g_ref, rb_ref, kc_ref, vc_ref, l1q_ref, l1v_ref, l1r_ref, ng_ref, o_ref, r_s = refs
    else:
        q_ref, k_ref, v_ref, g_ref, rb_ref, l1q_ref, l1v_ref, l1r_ref, ng_ref, o_ref, r_s = refs
    hq, hv, bd = _ret_masks()
    lgq = jnp.log1p(-jnp.exp(l1q_ref[...]))
    lgv = jnp.log1p(-jnp.exp(l1v_ref[...]))
    lgr = jnp.log1p(-jnp.exp(l1r_ref[...]))

    @pl.when(pl.program_id(1) == 0)
    def _():
        if has_ctx:
            r_s[...] = _ret_ctx_state(kc_ref[0], vc_ref[0], lgq[0:1], bd, reverse=False)
        else:
            r_s[...] = jnp.zeros_like(r_s)

    q = q_ref[0] * (RET_QK_DIM ** -0.5)
    k = k_ref[0]
    v = v_ref[0]
    c = q.shape[0]
    ii = lax.broadcasted_iota(jnp.int32, (c, c), 0)
    jj = lax.broadcasted_iota(jnp.int32, (c, c), 1)
    rel = (ii - jj).astype(F32)
    heads = range(RET_HEADS)
    kb = k.astype(BF16)
    vb = v.astype(BF16)
    sc = [_dot_nt(jnp.where(hq == h, q, 0.0), kb) for h in heads]
    d_f = [jnp.where(ii >= jj, jnp.exp(jnp.maximum(rel, 0.0) * lgr[h:h + 1, :]), 0.0) for h in heads]
    d_b = [jnp.where(jj >= ii, jnp.exp(jnp.maximum(-rel, 0.0) * lgr[4 + h:5 + h, :]), 0.0) for h in heads]
    pv = [jnp.dot((sc[h] * (d_f[h] + d_b[h])).astype(BF16), vb, preferred_element_type=F32) for h in heads]
    o = sum(jnp.where(hv == h, pv[h], 0.0) for h in heads)
    pos = lax.broadcasted_iota(jnp.int32, (c, 128), 0).astype(F32)
    r_old = r_s[...]
    o = o + _dot(q * jnp.exp((pos + 1.0) * lgq[0:1]), r_old) + _dot(q * jnp.exp((c - pos) * lgq[1:2]), rb_ref[0, 0])
    r_s[...] = _ret_state_update(r_old, k, v, lgq[0:1], lgv[0:1], bd, reverse=False)
    ones64 = _group_ones(256, 6)
    mu = _group_sum(o, ones64) * (1.0 / RET_V_DIM)
    d = o - mu
    var = _group_sum(d * d, ones64) * (1.0 / RET_V_DIM)
    o_ref[0] = d * lax.rsqrt(var + NORM_EPS) * ng_ref[...] * _silu(g_ref[0])


def _retention(zc, zcc, log1m_gamma, norm_g):
    b, t, _ = zc.shape
    c = min(RET_TILE, t)
    assert t % c == 0
    nc = t // c
    has_ctx = zcc is not None
    l1q = jnp.tile(jnp.repeat(log1m_gamma, 16, axis=-1), (1, 2))
    l1v = jnp.repeat(log1m_gamma, 64, axis=-1)
    l1r = jnp.broadcast_to(log1m_gamma.reshape(8, 1), (8, c))
    col = lambda cb, w_, f: pl.BlockSpec((1, c, w_), lambda i, n: (i, f(n), cb))
    fwd = lambda n: n
    bwd = lambda n: nc - 1 - n
    ctx_specs, ctx_args = [], []
    if has_ctx:
        l = zcc.shape[1]
        ctx_specs = [pl.BlockSpec((1, l, 128), lambda i, n: (i, 0, 1)), pl.BlockSpec((1, l, 256), lambda i, n: (i, 0, 1))]
        ctx_args = [zcc, zcc]
    rb = pl.pallas_call(
        functools.partial(_ret_state_kernel, has_ctx=has_ctx),
        name="retention_state",
        grid=(b, nc),
        in_specs=[col(1, 128, bwd), col(1, 256, bwd)] + ctx_specs + [_const_spec((2, 128)), _const_spec((2, 256))],
        out_specs=pl.BlockSpec((1, 1, 128, 256), lambda i, n: (i, bwd(n), 0, 0)),
        out_shape=jax.ShapeDtypeStruct((b, nc, 128, 256), F32),
        scratch_shapes=[pltpu.VMEM((128, 256), F32)],
        compiler_params=_params("parallel", "arbitrary"),
    )(zc, zc, *ctx_args, l1q, l1v)
    return pl.pallas_call(
        functools.partial(_ret_main_kernel, has_ctx=has_ctx),
        name="retention_main",
        grid=(b, nc),
        in_specs=[col(0, 128, fwd), col(1, 128, fwd), col(1, 256, fwd), col(2, 256, fwd),
                  pl.BlockSpec((1, 1, 128, 256), lambda i, n: (i, n, 0, 0))] + ctx_specs
                 + [_const_spec((2, 128)), _const_spec((2, 256)), _const_spec((8, c)), _const_spec((1, 256))],
        out_specs=pl.BlockSpec((1, c, 256), lambda i, n: (i, n, 0)),
        out_shape=jax.ShapeDtypeStruct((b, t, 256), F32),
        scratch_shapes=[pltpu.VMEM((128, 256), F32)],
        compiler_params=_params("parallel", "arbitrary"),
    )(zc, zc, zc, zc, rb, *ctx_args, l1q, l1v, l1r, norm_g.reshape(1, 256))


def _mla_weights(w_uq, w_ukv):
    dq = MLA_NOPE_DIM + MLA_ROPE_DIM
    qa = np.full((MLA_HEADS, 128), -1, np.int32)
    qb = np.full((MLA_HEADS, 128), -1, np.int32)
    kk = np.full((MLA_HEADS, 128), -1, np.int32)
    vv = np.zeros((MLA_HEADS, 64), np.int32)
    for h in range(MLA_HEADS):
        qa[h, 0:96] = h * dq + np.arange(96)
        qb[h, 64:80] = h * dq + 64 + 16 + np.arange(16)
        qb[h, 80:96] = h * dq + 64 + np.arange(16)
        kk[h, 0:64] = h * 128 + np.arange(64)
        vv[h] = h * 128 + 64 + np.arange(64)

    def take(w, idx):
        idx = idx.reshape(-1)
        out = jnp.take(w, jnp.asarray(np.maximum(idx, 0)), axis=1) * jnp.asarray((idx >= 0).astype(np.float32))[None, :]
        return out.astype(BF16)

    return take(w_uq, qa), take(w_uq, qb), take(w_ukv, kk), take(w_ukv, vv)


def _mla_prep_kernel(*refs, rope):
    if rope:
        zd_ref, qn_ref, kn_ref, wqa_ref, wqb_ref, wk_ref, wv_ref, cd_ref, sd_ref, q_ref, k_ref, v_ref = refs
    else:
        zd_ref, qn_ref, kn_ref, wqa_ref, wk_ref, wv_ref, q_ref, k_ref, v_ref = refs
    zd = zd_ref[0]
    cq, ckv, kr = zd[:, 0:256], zd[:, 256:384], zd[:, 384:512]
    nq = cq * lax.rsqrt(jnp.mean(cq * cq, axis=-1, keepdims=True) + NORM_EPS) * qn_ref[...]
    nkv = ckv * lax.rsqrt(jnp.mean(ckv * ckv, axis=-1, keepdims=True) + NORM_EPS) * kn_ref[...]
    scale = (MLA_NOPE_DIM + MLA_ROPE_DIM) ** -0.5 * LOG2_E
    qa = _dot(nq, wqa_ref[...])
    if rope:
        qb = _dot(nq, wqb_ref[...])
        cd, sd = cd_ref[...], sd_ref[...]
    kall = _dot(nkv, wk_ref[...])
    for h in range(MLA_HEADS):
        sl = slice(h * 128, (h + 1) * 128)
        qh = qa[:, sl]
        if rope:
            qh = qh * cd + qb[:, sl] * sd
        q_ref[0, :, sl] = (qh * scale).astype(BF16)
        k_ref[0, :, sl] = (kall[:, sl] + kr).astype(BF16)
    v_ref[0] = _dot(nkv, wv_ref[...]).astype(BF16)


def _mla_prep(zd, q_norm, kv_norm, weights, tables):
    b, t, _ = zd.shape
    tm = min(TOK_TILE, t)
    rope = tables is not None
    wqa, wqb, wk, wv = weights
    tok = lambda w_: pl.BlockSpec((1, tm, w_), lambda i, j: (i, j, 0))
    tab = pl.BlockSpec((tm, 128), lambda i, j: (j, 0))
    in_specs = [tok(ZD_W), _const_spec((1, 256)), _const_spec((1, 128)), _const_spec((256, 512))]
    args = [zd, q_norm.reshape(1, 256), kv_norm.reshape(1, 128), wqa]
    if rope:
        in_specs.append(_const_spec((256, 512)))
        args.append(wqb)
    in_specs += [_const_spec((128, 512)), _const_spec((128, 256))]
    args += [wk, wv]
    if rope:
        in_specs += [tab, tab]
        args += list(tables)
    return pl.pallas_call(
        functools.partial(_mla_prep_kernel, rope=rope),
        name="mla_prep",
        grid=(b, t // tm),
        in_specs=in_specs,
        out_specs=[tok(512), tok(512), tok(256)],
        out_shape=[jax.ShapeDtypeStruct((b, t, 512), BF16), jax.ShapeDtypeStruct((b, t, 512), BF16),
                   jax.ShapeDtypeStruct((b, t, 256), BF16)],
        compiler_params=_params("parallel", "parallel"),
    )(*args)


def _mla_attn_kernel(*refs, two_segments):
    scr = refs[-8:]
    s_scr, p_scr, m_scr, l_scr = scr[0:2], scr[2:4], scr[4:6], scr[6:8]
    if two_segments:
        q_ref, k1_ref, v1_ref, k2_ref, v2_ref, o_ref = refs[:-8]
        segments = [(k1_ref, v1_ref), (k2_ref, v2_ref)]
    else:
        q_ref, k2_ref, v2_ref, o_ref = refs[:-8]
        segments = [(k2_ref, v2_ref)]
    chunks = []
    off = 0
    for k_ref, v_ref in segments:
        n = k_ref.shape[1]
        step = min(MLA_KEY_CHUNK, n)
        for r0 in range(0, n, step):
            chunks.append((k_ref, v_ref, r0, step, off))
            off += step
    tq = q_ref.shape[1]
    lane = _lane_iota(256)
    nt_dims = (((1,), (1,)), ((), ()))

    def scores(h, ci):
        k_ref, _, r0, rows, col = chunks[ci]
        sl = slice(h * 128, (h + 1) * 128)
        s = lax.dot_general(q_ref[0, :, sl], k_ref[0, r0:r0 + rows, sl], nt_dims, preferred_element_type=F32)
        s_scr[h % 2][:, col:col + rows] = s
        cm = s[:, 0:128]
        for j in range(1, rows // 128):
            cm = jnp.maximum(cm, s[:, j * 128:(j + 1) * 128])
        m_ref = m_scr[h % 2]
        m_ref[...] = cm if ci == 0 else jnp.maximum(m_ref[...], cm)

    def row_max(h):
        m_ref = m_scr[h % 2]
        m_ref[...] = jnp.broadcast_to(jnp.max(m_ref[...], axis=-1, keepdims=True), (tq, 128))

    def probabilities(h, ci):
        _, _, _, rows, col = chunks[ci]
        s_ref, p_ref, m_ref, l_ref = s_scr[h % 2], p_scr[h % 2], m_scr[h % 2], l_scr[h % 2]
        for rb in range(0, tq, MLA_ROW_BLOCK):
            rsl = slice(rb, rb + MLA_ROW_BLOCK)
            mb = m_ref[rsl, :]
            lsum = None
            for j in range(rows // 128):
                csl = slice(col + j * 128, col + (j + 1) * 128)
                p = jnp.exp2(s_ref[rsl, csl] - mb)
                lsum = p if lsum is None else lsum + p
                p_ref[rsl, csl] = p.astype(BF16)
            l_ref[rsl, :] = lsum if ci == 0 else l_ref[rsl, :] + lsum

    def weighted_values(h):
        o = None
        col = 0
        for _, v_ref in segments:
            n = v_ref.shape[1]
            t = jnp.dot(p_scr[h % 2][:, col:col + n], v_ref[0], preferred_element_type=F32)
            o = t if o is None else o + t
            col += n
        den = jnp.sum(l_scr[h % 2][...], axis=-1, keepdims=True)
        return o * jnp.where((lane >> 6) == h, 1.0 / den, 0.0)

    for ci in range(len(chunks)):
        scores(0, ci)
    row_max(0)
    total = None
    for h in range(MLA_HEADS):
        for ci in range(len(chunks)):
            if h + 1 < MLA_HEADS:
                scores(h + 1, ci)
            probabilities(h, ci)
        if h + 1 < MLA_HEADS:
            row_max(h + 1)
        o = weighted_values(h)
        total = o if total is None else total + o
    o_ref[0] = total


def _mla_attend(q, kv_latent, kv_ctx):
    b, t, _ = q.shape
    tq = min(MLA_TQ, t)
    two = kv_latent is not None
    full = lambda n, w_: pl.BlockSpec((1, n, w_), lambda i, j: (i, 0, 0))
    in_specs = [pl.BlockSpec((1, tq, 512), lambda i, j: (i, j, 0))]
    args = [q]
    n_keys = 0
    for kv in ((kv_latent, kv_ctx) if two else (kv_ctx,)):
        n = kv[0].shape[1]
        assert n % min(MLA_KEY_CHUNK, n) == 0 and min(MLA_KEY_CHUNK, n) % 128 == 0
        n_keys += n
        in_specs += [full(n, 512), full(n, 256)]
        args += list(kv)
    return pl.pallas_call(
        functools.partial(_mla_attn_kernel, two_segments=two),
        name="mla_attention",
        grid=(b, t // tq),
        in_specs=in_specs,
        out_specs=pl.BlockSpec((1, tq, 256), lambda i, j: (i, j, 0)),
        out_shape=jax.ShapeDtypeStruct((b, t, 256), F32),
        scratch_shapes=[pltpu.VMEM((tq, n_keys), F32)] * 2 + [pltpu.VMEM((tq, n_keys), BF16)] * 2
                       + [pltpu.VMEM((tq, 128), F32)] * 4,
        compiler_params=_params("parallel", "arbitrary"),
    )(*args)


def _proj_mlp_kernel(ya_ref, yb_ref, yc_ref, yd_ref, x_ref, g1_ref, sc_ref, sh_ref, g2_ref, wo_ref, w1_ref, w2_ref,
                     l1g_ref, l1b_ref, l2g_ref, l2b_ref, o_ref):
    y = jnp.concatenate([ya_ref[0], yb_ref[0], yc_ref[0], yd_ref[0]], axis=-1).astype(BF16)
    proj = jnp.dot(y, wo_ref[...], preferred_element_type=F32)
    x1 = _layer_norm(DEEPNORM_ALPHA * x_ref[0] + g1_ref[0] * proj, l1g_ref[...], l1b_ref[...])
    h = (x1 * (1.0 + sc_ref[0]) + sh_ref[0]).astype(BF16)
    acc = None
    for j in range(D_FF // FF_CHUNK):
        cols = slice(j * FF_CHUNK, (j + 1) * FF_CHUNK)
        u = jnp.maximum(jnp.dot(h, w1_ref[:, cols], preferred_element_type=F32), 0.0)
        t = jnp.dot((u * u).astype(BF16), w2_ref[cols, :], preferred_element_type=F32)
        acc = t if acc is None else acc + t
    o_ref[0] = _layer_norm(DEEPNORM_ALPHA * x1 + g2_ref[0] * acc, l2g_ref[...], l2b_ref[...])


def _proj_mlp(ys, x, g1, sc2, sh2, g2, w_out, w1, w2, ln1_g, ln1_b, ln2_g, ln2_b):
    b, t, d = x.shape
    tm = min(TOK_TILE, t)
    tok = lambda w_: pl.BlockSpec((1, tm, w_), lambda i, j: (i, j, 0))
    vec = pl.BlockSpec((1, 1, d), lambda i, j: (i, 0, 0))
    row = lambda v: v.reshape(1, d)
    return pl.pallas_call(
        _proj_mlp_kernel,
        name="out_proj_mlp",
        grid=(b, t // tm),
        in_specs=[tok(256)] * 4 + [tok(d), vec, vec, vec, vec, _resident_spec((d, d)), _resident_spec((d, D_FF)),
                                   _resident_spec((D_FF, d))] + [_const_spec((1, d))] * 4,
        out_specs=tok(d),
        out_shape=jax.ShapeDtypeStruct((b, t, d), F32),
        compiler_params=_params("parallel", "parallel"),
    )(*ys, x, g1, sc2, sh2, g2, w_out, w1, w2, row(ln1_g), row(ln1_b), row(ln2_g), row(ln2_b))


def _rope_tables(n_tok):
    rows = n_tok // GRID_W
    freqs = lambda dim: ROPE_BASE ** (-jnp.arange(0, dim, 2, dtype=F32) / dim)
    row = jnp.broadcast_to(jnp.arange(rows, dtype=F32)[:, None], (rows, GRID_W)).reshape(-1)
    col = jnp.broadcast_to(jnp.arange(GRID_W, dtype=F32)[None, :], (rows, GRID_W)).reshape(-1)

    def axial(rot_dim):
        inv = freqs(rot_dim // 2)
        return jnp.concatenate([row[:, None] * inv, col[:, None] * inv], axis=-1)

    ang_a = axial(SWA_HEAD_DIM)
    ang_c = jnp.arange(n_tok, dtype=F32)[:, None] * freqs(RET_QK_DIM)
    ang_d = axial(MLA_ROPE_DIM)
    pair = lambda a, reps: (jnp.tile(jnp.cos(a), (1, 2 * reps)),
                            jnp.concatenate([-jnp.tile(jnp.sin(a), (1, reps)), jnp.tile(jnp.sin(a), (1, reps))], axis=-1))
    ca, sa = pair(ang_a, 2)
    cc, sc = pair(ang_c, 4)
    one = jnp.ones((n_tok, 64), F32)
    zero = jnp.zeros((n_tok, 64), F32)
    cd = jnp.concatenate([one, jnp.cos(ang_d), jnp.cos(ang_d), one[:, :32]], axis=-1)
    sd = jnp.concatenate([zero, -jnp.sin(ang_d), jnp.sin(ang_d), zero[:, :32]], axis=-1)
    return ca, sa, cc, sc, cd, sd


def kernel(x, c, ctx, c_ctx, ada_w, ada_b, w_in, swa_sink, dn_conv_w, dn_a_log, dn_dt_bias, dn_norm_g,
           ret_log1m_gamma, ret_norm_g, mla_q_norm, mla_w_uq, mla_kv_norm, mla_w_ukv, w_out, ln1_g, ln1_b,
           w_ff1, w_ff2, ln2_g, ln2_b):
    b, n_tok, d = x.shape
    tables = _rope_tables(n_tok)
    cc = jnp.concatenate([c, c_ctx[None, :], jnp.zeros((7, d), F32)], axis=0)
    perm_a = np.concatenate([np.arange(64) + 64 * hd for hd in (0, 2, 1, 3)])
    out_rows = np.concatenate([perm_a, np.arange(256, 1024)])
    xc = ctx
    for layer in range(DEPTH):
        with_ctx_out = layer < DEPTH - 1
        mod = _ada_modulation(cc, ada_w[layer], ada_b[layer])
        sh1, sc1, g1, sh2, sc2, g2 = jnp.split(mod[:b, None, :], 6, axis=-1)
        cmod = jnp.broadcast_to(mod[b][None, None, :], (b, 1, 6 * d))
        csh1, csc1, cg1, csh2, csc2, cg2 = jnp.split(cmod, 6, axis=-1)
        w_in_p = _permute_in_proj(w_in[layer])
        za, zb, zc, zd = _in_proj(x, sc1, sh1, w_in_p, tables)
        zac, zbc, zcc, zdc = _in_proj(xc, csc1, csh1, w_in_p, None)
        sink = swa_sink[layer]
        ya = _swa_local(za, zac, sink)
        yb, ybc = _deltanet(zb, zbc, dn_conv_w[layer], dn_a_log[layer], dn_dt_bias[layer], dn_norm_g[layer],
                            with_ctx_out)
        yr = _retention(zc, zcc, ret_log1m_gamma[layer], ret_norm_g[layer])
        mla_w = _mla_weights(mla_w_uq[layer], mla_w_ukv[layer])
        qd, kd, vd = _mla_prep(zd, mla_q_norm[layer], mla_kv_norm[layer], mla_w, tables[4:6])
        qdc, kdc, vdc = _mla_prep(zdc, mla_q_norm[layer], mla_kv_norm[layer], mla_w, None)
        yd = _mla_attend(qd, (kd, vd), (kdc, vdc))
        w_out_p = jnp.take(w_out[layer], jnp.asarray(out_rows), axis=0).astype(BF16)
        w1 = w_ff1[layer].astype(BF16)
        w2 = w_ff2[layer].astype(BF16)
        lns = (ln1_g[layer], ln1_b[layer], ln2_g[layer], ln2_b[layer])
        x_new = _proj_mlp((ya, yb, yr, yd), x, g1, sc2, sh2, g2, w_out_p, w1, w2, *lns)
        if with_ctx_out:
            yac = _swa_ctx(zac, sink)
            yrc = _retention(zcc, None, ret_log1m_gamma[layer], ret_norm_g[layer])
            ydc = _mla_attend(qdc, None, (kdc, vdc))
            xc = _proj_mlp((yac, ybc, yrc, ydc), xc, cg1, csc2, csh2, cg2, w_out_p, w1, w2, *lns)
        x = x_new
    return x
```
